```python
import math
import jax, jax.numpy as jnp
from jax import lax
import numpy as np

D_MODEL = 1024
BATCH = 4
SEQ = 8192
DEPTH = 4

N_META = 16
ATT_HEAD_DIM = 64
ATT_HEADS = D_MODEL // ATT_HEAD_DIM
ATT_KV_HEADS = ATT_HEADS // 8
ATT_WIDTH = ATT_HEADS * ATT_HEAD_DIM
ATT_KV_WIDTH = ATT_KV_HEADS * ATT_HEAD_DIM
WINDOW = 128
ATT_BLOCK = 128
ROT_DIM = ATT_HEAD_DIM // 4
ROPE_THETA = 500000.0
M_HEADS = 4
M_V_DIM = D_MODEL // M_HEADS
M_QK_DIM = M_V_DIM // 2
M_WIDTH = M_HEADS * M_V_DIM
M_QK_WIDTH = M_HEADS * M_QK_DIM
CHUNK = 64
CONV_K = 4
MIX_WIDTH = ATT_WIDTH + M_WIDTH
SPLIT_SIZES = (ATT_WIDTH, ATT_KV_WIDTH, ATT_KV_WIDTH, ATT_WIDTH,
               M_QK_WIDTH, M_QK_WIDTH, M_WIDTH, M_HEADS, M_HEADS, M_WIDTH, M_WIDTH)
IN_WIDTH = sum(SPLIT_SIZES)
EPS = 1e-6
NEG = -1e30

kernel_name = "hymba_swa_sink_mlstm_hybrid"


def split_points():
    pts, acc = [], 0
    for s in SPLIT_SIZES[:-1]:
        acc += s
        pts.append(acc)
    return pts


def rms_norm(x, g):
    xf = x.astype(jnp.float32)
    y = xf * lax.rsqrt(jnp.mean(xf * xf, axis=-1, keepdims=True) + EPS)
    return (y * g.astype(jnp.float32)).astype(x.dtype)


def rope_tables(length):
    pos = jnp.arange(length, dtype=jnp.float32)
    inv_freq = ROPE_THETA ** (-jnp.arange(0, ROT_DIM, 2, dtype=jnp.float32) / ROT_DIM)
    ang = pos[:, None] * inv_freq[None, :]
    return jnp.cos(ang), jnp.sin(ang)


def apply_partial_rope(x, cos, sin):
    half = ROT_DIM // 2
    x1, x2, rest = x[..., :half], x[..., half:ROT_DIM], x[..., ROT_DIM:]
    c = cos[None, :, None, :]
    s = sin[None, :, None, :]
    out = jnp.concatenate([x1 * c - x2 * s, x2 * c + x1 * s, rest.astype(jnp.float32)], axis=-1)
    return out.astype(x.dtype)


def sliding_window_sink_attention(q, k, v, sink):
    B, L, H, Dh = q.shape
    KV = k.shape[2]
    G = H // KV
    pad = ATT_BLOCK - N_META
    Lp = L + pad
    nb = Lp // ATT_BLOCK
    padf = lambda a: jnp.pad(a, ((0, 0), (pad, 0), (0, 0), (0, 0)))
    qb = padf(q).reshape(B, nb, ATT_BLOCK, KV, G, Dh)

    def band_keys(a):
        ab = padf(a).reshape(B, nb, ATT_BLOCK, KV, Dh)
        prev = jnp.concatenate([jnp.zeros_like(ab[:, :1]), ab[:, :-1]], axis=1)
        meta = jnp.broadcast_to(a[:, None, :N_META], (B, nb, N_META, KV, Dh))
        return jnp.concatenate([meta, prev, ab], axis=2)

    kb = band_keys(k)
    vb = band_keys(v)

    blk = jnp.arange(nb)
    qpos = blk[:, None] * ATT_BLOCK + jnp.arange(ATT_BLOCK)[None, :] - pad
    kpos = blk[:, None] * ATT_BLOCK - ATT_BLOCK + jnp.arange(2 * ATT_BLOCK)[None, :] - pad
    mpos = jnp.arange(N_META)
    valid_meta = mpos[None, None, :] <= qpos[:, :, None]
    dist = qpos[:, :, None] - kpos[:, None, :]
    valid_band = (kpos[:, None, :] >= N_META) & (dist >= 0) & (dist < WINDOW)
    mask = jnp.concatenate([valid_meta, valid_band], axis=-1)

    scale = 1.0 / math.sqrt(Dh)
    scores = jnp.einsum('bnqkgd,bnskd->bnkgqs', qb, kb).astype(jnp.float32) * scale
    scores = jnp.where(mask[None, :, None, None], scores, NEG)
    sink_b = sink.astype(jnp.float32).reshape(KV, G)[None, None, :, :, None, None]
    m = jnp.maximum(scores.max(axis=-1, keepdims=True), sink_b)
    e = jnp.exp(scores - m)
    probs = e / (e.sum(axis=-1, keepdims=True) + jnp.exp(sink_b - m))
    out = jnp.einsum('bnkgqs,bnskd->bnqkgd', probs.astype(vb.dtype), vb)
    return out.reshape(B, Lp, H, Dh)[:, pad:]


def causal_depthwise_conv(x, w, b):
    C = x.shape[-1]
    y = lax.conv_general_dilated(x, w[:, None, :].astype(x.dtype), (1,), [(CONV_K - 1, 0)],
                                 dimension_numbers=('NWC', 'WIO', 'NWC'), feature_group_count=C)
    return y + b.astype(x.dtype)


def mlstm_chunkwise(q, k, v, log_i, log_f):
    B, L, NH, DQK = q.shape
    DV = v.shape[-1]
    pad = CHUNK - N_META
    Lp = L + pad
    NC = Lp // CHUNK
    f32 = jnp.float32

    def prep(a):
        a = jnp.pad(a.astype(f32), ((0, 0), (pad, 0), (0, 0), (0, 0)))
        return a.reshape(B, NC, CHUNK, NH, a.shape[-1]).transpose(0, 3, 1, 2, 4)

    q, k, v = prep(q), prep(k), prep(v)
    log_i = jnp.pad(log_i, ((0, 0), (pad, 0), (0, 0)), constant_values=NEG)
    log_f = jnp.pad(log_f, ((0, 0), (pad, 0), (0, 0)), constant_values=0.0)
    log_i = log_i.reshape(B, NC, CHUNK, NH).transpose(0, 3, 1, 2)
    log_f = log_f.reshape(B, NC, CHUNK, NH).transpose(0, 3, 1, 2)

    b = jnp.cumsum(log_f, axis=-1)
    bT = b[..., -1]
    causal = jnp.tril(jnp.ones((CHUNK, CHUNK), dtype=bool))
    D = jnp.where(causal, b[..., :, None] - b[..., None, :] + log_i[..., None, :], NEG)

    g = bT[..., None] - b + log_i
    mg = g.max(axis=-1)
    wg = jnp.exp(g - mg[..., None])
    kv_chunk = jnp.einsum('bhcsk,bhcsv->bhckv', wg[..., None] * k, v)
    n_chunk = jnp.einsum('bhcs,bhcsk->bhck', wg, k)

    def step(carry, inp):
        C, n, m = carry
        kv_c, n_c, bT_c, mg_c = inp
        m_new = jnp.maximum(bT_c + m, mg_c)
        a = jnp.exp(bT_c + m - m_new)
        w = jnp.exp(mg_c - m_new)
        C_new = a[..., None, None] * C + w[..., None, None] * kv_c
        n_new = a[..., None] * n + w[..., None] * n_c
        return (C_new, n_new, m_new), (C, n, m)

    init = (jnp.zeros((B, NH, DQK, DV), f32), jnp.zeros((B, NH, DQK), f32), jnp.zeros((B, NH), f32))
    xs = (jnp.moveaxis(kv_chunk, 2, 0), jnp.moveaxis(n_chunk, 2, 0),
          jnp.moveaxis(bT, 2, 0), jnp.moveaxis(mg, 2, 0))
    _, (C_in, n_in, m_in) = lax.scan(step, init, xs)
    C_in = jnp.moveaxis(C_in, 0, 2)
    n_in = jnp.moveaxis(n_in, 0, 2)
    m_in = jnp.moveaxis(m_in, 0, 2)

    a_t = b + m_in[..., None]
    m_t = jnp.maximum(a_t, D.max(axis=-1))
    S = jnp.einsum('bhctd,bhcsd->bhcts', q, k) * jnp.exp(D - m_t[..., None])
    inter = jnp.exp(a_t - m_t)
    num = inter[..., None] * jnp.einsum('bhctk,bhckv->bhctv', q, C_in) \
        + jnp.einsum('bhcts,bhcsv->bhctv', S, v)
    den = inter * jnp.einsum('bhctk,bhck->bhct', q, n_in) + S.sum(axis=-1)
    h = num / jnp.maximum(jnp.abs(den), jnp.exp(-m_t))[..., None]
    h = h.transpose(0, 2, 3, 1, 4).reshape(B, Lp, NH, DV)
    return h[:, pad:]


def hybrid_layer(x, cos, sin, norm_g, w_in, q_norm_g, k_norm_g, sink,
                 conv_w, conv_b, b_i, b_f, out_norm_g, w_out):
    B, L, _ = x.shape
    h = rms_norm(x, norm_g)
    z = h @ w_in.astype(x.dtype)
    aq, ak, av, ag, mq, mk, mv, mi, mf, mo, mgate = jnp.split(z, split_points(), axis=-1)

    q = rms_norm(aq.reshape(B, L, ATT_HEADS, ATT_HEAD_DIM), q_norm_g)
    k = rms_norm(ak.reshape(B, L, ATT_KV_HEADS, ATT_HEAD_DIM), k_norm_g)
    q = apply_partial_rope(q, cos, sin)
    k = apply_partial_rope(k, cos, sin)
    v = av.reshape(B, L, ATT_KV_HEADS, ATT_HEAD_DIM)
    ya = sliding_window_sink_attention(q, k, v, sink).reshape(B, L, ATT_WIDTH)
    ya = ya * jax.nn.silu(ag)

    qk = jax.nn.silu(causal_depthwise_conv(jnp.concatenate([mq, mk], axis=-1), conv_w, conv_b))
    mq, mk = jnp.split(qk, [M_QK_WIDTH], axis=-1)
    mq = mq.reshape(B, L, M_HEADS, M_QK_DIM)
    mk = mk.reshape(B, L, M_HEADS, M_QK_DIM) * (M_QK_DIM ** -0.5)
    mv = mv.reshape(B, L, M_HEADS, M_V_DIM)
    log_i = (mi + b_i).astype(jnp.float32)
    log_f = jax.nn.log_sigmoid((mf + b_f).astype(jnp.float32))
    hm = mlstm_chunkwise(mq, mk, mv, log_i, log_f)
    hm = rms_norm(hm, out_norm_g.reshape(M_HEADS, M_V_DIM)).reshape(B, L, M_WIDTH).astype(x.dtype)
    ym = hm * jax.nn.sigmoid(mo) * jax.nn.silu(mgate)

    y = jnp.concatenate([ya, ym], axis=-1) @ w_out.astype(x.dtype)
    return x + y


def setup_inputs(seed: int = 0) -> dict:
    key = jax.random.key(seed)
    ks = jax.random.split(key, 13)
    f32 = jnp.float32
    n = lambda k, shape, s: jax.random.normal(k, shape, f32) * s
    return {
        "x": n(ks[0], (BATCH, SEQ, D_MODEL), 1.0),
        "meta": n(ks[1], (N_META, D_MODEL), 1.0),
        "norm_g": 1.0 + n(ks[2], (DEPTH, D_MODEL), 0.02),
        "w_in": n(ks[3], (DEPTH, D_MODEL, IN_WIDTH), D_MODEL ** -0.5),
        "attn_q_norm_g": 1.0 + n(ks[4], (DEPTH, ATT_HEAD_DIM), 0.02),
        "attn_k_norm_g": 1.0 + n(ks[5], (DEPTH, ATT_HEAD_DIM), 0.02),
        "attn_sink": n(ks[6], (DEPTH, ATT_HEADS), 0.5),
        "mlstm_conv_w": n(ks[7], (DEPTH, CONV_K, 2 * M_QK_WIDTH), CONV_K ** -0.5),
        "mlstm_conv_b": n(ks[8], (DEPTH, 2 * M_QK_WIDTH), 0.02),
        "mlstm_b_i": n(ks[9], (DEPTH, M_HEADS), 0.1),
        "mlstm_b_f": jnp.linspace(3.0, 6.0, M_HEADS, dtype=f32)[None, :] + n(ks[10], (DEPTH, M_HEADS), 0.1),
        "mlstm_out_norm_g": 1.0 + n(ks[11], (DEPTH, M_WIDTH), 0.02),
        "w_out": n(ks[12], (DEPTH, MIX_WIDTH, D_MODEL), 0.5 * MIX_WIDTH ** -0.5),
    }


def reference(x, meta, norm_g, w_in, attn_q_norm_g, attn_k_norm_g, attn_sink,
              mlstm_conv_w, mlstm_conv_b, mlstm_b_i, mlstm_b_f, mlstm_out_norm_g, w_out):
    B = x.shape[0]
    h = jnp.concatenate([jnp.broadcast_to(meta.astype(x.dtype)[None], (B, N_META, x.shape[-1])), x], axis=1)
    L = h.shape[1]
    cos, sin = rope_tables(L)
    for l in range(DEPTH):
        h = hybrid_layer(h, cos, sin, norm_g[l], w_in[l], attn_q_norm_g[l], attn_k_norm_g[l],
                         attn_sink[l], mlstm_conv_w[l], mlstm_conv_b[l], mlstm_b_i[l],
                         mlstm_b_f[l], mlstm_out_norm_g[l], w_out[l])
    return h[:, N_META:]
```

```python
import functools

import jax
import jax.numpy as jnp
from jax import lax
from jax.experimental import pallas as pl
from jax.experimental.pallas import tpu as pltpu

D_MODEL = 1024
N_META = 16
HEAD_DIM = 64
ATT_HEADS = 16
KV_HEADS = 2
ATT_BLOCK = 128
ROT_DIM = 16
ROPE_THETA = 500000.0
M_HEADS = 4
M_V_DIM = 256
M_QK_DIM = 128
CHUNK = 64
CONV_K = 4
EPS = 1e-6
NEG = -1e30

LANES = 128
SUBLANES = 8
PAD = ATT_BLOCK - N_META
HEAD_PAIRS = ATT_HEADS // 2
PAIRS_PER_KV = HEAD_PAIRS // KV_HEADS
KEY_ROWS = 5 * ATT_BLOCK
META_ROW = 4 * ATT_BLOCK

OFF_Q = 0
OFF_K = 1024
OFF_V = 1152
OFF_AG = 1280
OFF_MQ = 2304
OFF_MK = 2816
OFF_MV = 3328
OFF_MO = 4352
OFF_MG = 5376
OFF_GI = 6400
OFF_GF = 6528
IN_COLS = 6656
QK_CONV_COLS = 2 * M_HEADS * M_QK_DIM
MIX_COLS = 2 * D_MODEL

ROW_TILE = 128
VMEM_LIMIT_BYTES = 56 * 1024 * 1024

F32 = jnp.float32
BF16 = jnp.bfloat16


def _sigmoid(x):
    return 1.0 / (1.0 + jnp.exp(-x))


def _silu(x):
    return x * _sigmoid(x)


def _log_sigmoid(x):
    return jnp.minimum(x, 0.0) - jnp.log1p(jnp.exp(-jnp.abs(x)))


def _dot(a, b):
    return jnp.dot(a, b, preferred_element_type=F32)


def _dot_nt(a, b):
    return lax.dot_general(a, b, (((1,), (1,)), ((), ())), preferred_element_type=F32)


def _dot_tn(a, b):
    return lax.dot_general(a, b, (((0,), (0,)), ((), ())), preferred_element_type=F32)


def _row_scan(x, row, combine, fill):
    shift = 1
    while shift < x.shape[0]:
        x = combine(x, jnp.where(row >= shift, pltpu.roll(x, shift, axis=0), fill))
        shift *= 2
    return x


def _layer_kernel(sink_ref, h_ref, rc_ref, rs_ref, ng_ref, win_ref, qg_ref, kg_ref,
                  cw_ref, cb_ref, bi_ref, bf_ref, og_ref, wout_ref, o_ref,
                  z_ref, kk_ref, vv_ref, p_ref, u_ref, y_ref, c_ref, n_ref, m_ref, *, tb):
    i = pl.program_id(1)
    nblk = tb // ATT_BLOCK
    nchunk = tb // CHUNK

    @pl.when(i == 0)
    def _():
        kk_ref[...] = jnp.zeros_like(kk_ref)
        vv_ref[...] = jnp.zeros_like(vv_ref)
        u_ref[0:SUBLANES, :] = jnp.zeros((SUBLANES, QK_CONV_COLS), F32)
        c_ref[...] = jnp.zeros_like(c_ref)
        n_ref[...] = jnp.zeros_like(n_ref)
        m_ref[...] = jnp.zeros_like(m_ref)

    h = h_ref[0]
    hn = h * lax.rsqrt(jnp.mean(h * h, axis=-1, keepdims=True) + EPS) * ng_ref[...]
    z_ref[...] = _dot(hn.astype(BF16), win_ref[...])

    lane = lax.broadcasted_iota(jnp.int32, (ATT_BLOCK, LANES), 1)
    lo_half = lane < HEAD_DIM
    rot_lo = (lane % HEAD_DIM) < (ROT_DIM // 2)

    def norm_rope(xp, gain, rc, rs):
        sq = xp * xp
        ss_lo = jnp.sum(jnp.where(lo_half, sq, 0.0), axis=-1, keepdims=True)
        ss_hi = jnp.sum(jnp.where(lo_half, 0.0, sq), axis=-1, keepdims=True)
        rinv = jnp.where(lo_half, lax.rsqrt(ss_lo * (1.0 / HEAD_DIM) + EPS),
                         lax.rsqrt(ss_hi * (1.0 / HEAD_DIM) + EPS))
        xn = xp * rinv * gain
        partner = jnp.where(rot_lo, pltpu.roll(xn, LANES - ROT_DIM // 2, axis=1),
                            pltpu.roll(xn, ROT_DIM // 2, axis=1))
        return xn * rc + partner * rs

    for j in range(nblk):
        r0 = j * ATT_BLOCK
        blk = i * nblk + j
        rc = rc_ref[r0:r0 + ATT_BLOCK, :]
        rs = rs_ref[r0:r0 + ATT_BLOCK, :]

        kp = norm_rope(z_ref[r0:r0 + ATT_BLOCK, OFF_K:OFF_K + LANES], kg_ref[...], rc, rs)
        vp = z_ref[r0:r0 + ATT_BLOCK, OFF_V:OFF_V + LANES]
        kp_sw = pltpu.roll(kp, HEAD_DIM, axis=1)
        vp_sw = pltpu.roll(vp, HEAD_DIM, axis=1)
        for c in range(KV_HEADS):
            for ref, own, swapped in ((kk_ref, kp, kp_sw), (vv_ref, vp, vp_sw)):
                ref[c, 0:ATT_BLOCK, :] = ref[c, ATT_BLOCK:2 * ATT_BLOCK, :]
                ref[c, 2 * ATT_BLOCK:3 * ATT_BLOCK, :] = ref[c, 3 * ATT_BLOCK:4 * ATT_BLOCK, :]
                top = jnp.where(lo_half, own if c == 0 else swapped, 0.0).astype(BF16)
                bot = jnp.where(lo_half, 0.0, swapped if c == 0 else own).astype(BF16)
                ref[c, ATT_BLOCK:2 * ATT_BLOCK, :] = top
                ref[c, 3 * ATT_BLOCK:4 * ATT_BLOCK, :] = bot

                @pl.when(blk == 0)
                def _(ref=ref, top=top, bot=bot, c=c):
                    ref[c, META_ROW:META_ROW + N_META, :] = top[PAD:ATT_BLOCK, :]
                    ref[c, META_ROW + N_META:META_ROW + 2 * N_META, :] = bot[PAD:ATT_BLOCK, :]

        qrow = blk * ATT_BLOCK + lax.broadcasted_iota(jnp.int32, (ATT_BLOCK, 2 * ATT_BLOCK), 0)
        krow = (blk - 1) * ATT_BLOCK + lax.broadcasted_iota(jnp.int32, (ATT_BLOCK, 2 * ATT_BLOCK), 1)
        dist = qrow - krow
        band_ok = (krow >= ATT_BLOCK) & (dist >= 0) & (dist < ATT_BLOCK)
        qrow_m = blk * ATT_BLOCK + lax.broadcasted_iota(jnp.int32, (ATT_BLOCK, LANES), 0)
        meta_ok = (PAD + (lane % N_META)) <= qrow_m
        meta_ok_a = (lane < N_META) & meta_ok
        meta_ok_b = (lane >= N_META) & (lane < 2 * N_META) & meta_ok

        for c in range(KV_HEADS):
            q4 = []
            for pi in range(PAIRS_PER_KV):
                p = c * PAIRS_PER_KV + pi
                qp = norm_rope(z_ref[r0:r0 + ATT_BLOCK, OFF_Q + p * LANES:OFF_Q + (p + 1) * LANES],
                               qg_ref[...], rc, rs)
                q4.append(qp.astype(BF16))
            s4 = _dot_nt(jnp.concatenate(q4, axis=0), kk_ref[c])
            for pi in range(PAIRS_PER_KV):
                p = c * PAIRS_PER_KV + pi
                s = s4[pi * ATT_BLOCK:(pi + 1) * ATT_BLOCK, :]
                sm = s[:, META_ROW:META_ROW + LANES]
                probs = []
                pm = None
                for hh, ok in ((0, meta_ok_a), (1, meta_ok_b)):
                    sink = sink_ref[2 * p + hh]
                    sb = jnp.where(band_ok, s[:, hh * 2 * ATT_BLOCK:(hh + 1) * 2 * ATT_BLOCK], NEG)
                    smh = jnp.where(ok, sm, NEG)
                    mx = jnp.maximum(jnp.maximum(jnp.max(sb, axis=-1, keepdims=True),
                                                 jnp.max(smh, axis=-1, keepdims=True)), sink)
                    eb = jnp.exp(sb - mx)
                    em = jnp.exp(smh - mx)
                    den = (jnp.sum(eb, axis=-1, keepdims=True) + jnp.sum(em, axis=-1, keepdims=True)
                           + jnp.exp(sink - mx))
                    rden = 1.0 / den
                    probs.append((eb * rden).astype(BF16))
                    pm = em * rden if pm is None else pm + em * rden
                rows = slice(pi * ATT_BLOCK, (pi + 1) * ATT_BLOCK)
                p_ref[rows, 0:2 * ATT_BLOCK] = probs[0]
                p_ref[rows, 2 * ATT_BLOCK:4 * ATT_BLOCK] = probs[1]
                p_ref[rows, META_ROW:META_ROW + LANES] = pm.astype(BF16)
            o4 = _dot(p_ref[...], vv_ref[c])
            for pi in range(PAIRS_PER_KV):
                p = c * PAIRS_PER_KV + pi
                gate = _silu(z_ref[r0:r0 + ATT_BLOCK, OFF_AG + p * LANES:OFF_AG + (p + 1) * LANES])
                y_ref[r0:r0 + ATT_BLOCK, p * LANES:(p + 1) * LANES] = (
                    o4[pi * ATT_BLOCK:(pi + 1) * ATT_BLOCK, :] * gate).astype(BF16)

    u_ref[SUBLANES:SUBLANES + tb, :] = z_ref[:, OFF_MQ:OFF_MQ + QK_CONV_COLS]
    crow = lax.broadcasted_iota(jnp.int32, (CHUNK, LANES), 0)
    tril = (lax.broadcasted_iota(jnp.int32, (CHUNK, CHUNK), 0)
            >= lax.broadcasted_iota(jnp.int32, (CHUNK, CHUNK), 1))
    for cj in range(nchunk):
        q0 = cj * CHUNK
        real = (i * tb + q0 + crow) >= PAD
        real_col = (i * tb + q0 + lax.broadcasted_iota(jnp.int32, (CHUNK, 1), 0)) >= PAD

        conv = cb_ref[...]
        for t in range(CONV_K):
            a0 = SUBLANES - (CONV_K - 1) + t + q0
            conv = conv + cw_ref[t:t + 1, :] * u_ref[a0:a0 + CHUNK, :]
        qk = jnp.where(real_col, _silu(conv), 0.0)

        li = jnp.where(real, z_ref[q0:q0 + CHUNK, OFF_GI:OFF_GI + LANES] + bi_ref[...], NEG)
        lf = jnp.where(real, _log_sigmoid(z_ref[q0:q0 + CHUNK, OFF_GF:OFF_GF + LANES] + bf_ref[...]), 0.0)
        b = _row_scan(lf, crow, jnp.add, 0.0)
        r = li - b
        rmax = jnp.max(r, axis=0, keepdims=True)
        wg = jnp.exp(r - rmax)
        dmax = b + _row_scan(r, crow, jnp.maximum, NEG)
        m_in = m_ref[0:1, :]
        a_t = b + m_in
        m_t = jnp.maximum(a_t, dmax)
        inter = jnp.exp(a_t - m_t)
        cvec = b - m_t
        em_t = jnp.exp(-m_t)
        r_t = r.T
        b_last = b[CHUNK - 1:CHUNK, :]
        mg = b_last + rmax
        m_new = jnp.maximum(b_last + m_in, mg)
        a_st = jnp.exp(b_last + m_in - m_new)
        w_st = jnp.exp(mg - m_new)
        m_ref[...] = jnp.broadcast_to(m_new, (SUBLANES, LANES))

        for hd in range(M_HEADS):
            q = qk[:, hd * M_QK_DIM:(hd + 1) * M_QK_DIM]
            k = qk[:, (M_HEADS + hd) * M_QK_DIM:(M_HEADS + hd + 1) * M_QK_DIM] * (M_QK_DIM ** -0.5)
            vb = z_ref[q0:q0 + CHUNK, OFF_MV + hd * M_V_DIM:OFF_MV + (hd + 1) * M_V_DIM].astype(BF16)
            qb = q.astype(BF16)
            c_in = c_ref[hd]
            n_in = n_ref[hd, 0:1, :]

            arg = cvec[:, hd:hd + 1] + r_t[hd:hd + 1, :]
            sg = _dot_nt(qb, k.astype(BF16)) * jnp.exp(jnp.where(tril, arg, NEG))
            inter_c = inter[:, hd:hd + 1]
            num = inter_c * _dot(qb, c_in.astype(BF16)) + _dot(sg.astype(BF16), vb)
            den = (inter_c * jnp.sum(q * n_in, axis=-1, keepdims=True)
                   + jnp.sum(sg, axis=-1, keepdims=True))
            hout = num / jnp.maximum(jnp.abs(den), em_t[:, hd:hd + 1])

            kw = wg[:, hd:hd + 1] * k
            a_s = a_st[:, hd:hd + 1]
            w_s = w_st[:, hd:hd + 1]
            c_ref[hd] = a_s * c_in + w_s * _dot_tn(kw.astype(BF16), vb)
            n_new = a_s * n_in + w_s * jnp.sum(kw, axis=0, keepdims=True)
            n_ref[hd] = jnp.broadcast_to(n_new, (SUBLANES, LANES))

            cols = slice(hd * M_V_DIM, (hd + 1) * M_V_DIM)
            hm = hout * lax.rsqrt(jnp.mean(hout * hout, axis=-1, keepdims=True) + EPS) * og_ref[:, cols]
            ocols = slice(OFF_MO + hd * M_V_DIM, OFF_MO + (hd + 1) * M_V_DIM)
            gcols = slice(OFF_MG + hd * M_V_DIM, OFF_MG + (hd + 1) * M_V_DIM)
            ym = hm * _sigmoid(z_ref[q0:q0 + CHUNK, ocols]) * _silu(z_ref[q0:q0 + CHUNK, gcols])
            y_ref[q0:q0 + CHUNK, D_MODEL + hd * M_V_DIM:D_MODEL + (hd + 1) * M_V_DIM] = ym.astype(BF16)

    u_ref[0:SUBLANES, :] = u_ref[tb:tb + SUBLANES, :]

    y = _dot(y_ref[...], wout_ref[...])
    orow = i * tb + lax.broadcasted_iota(jnp.int32, (tb, 1), 0)
    o_ref[0] = jnp.where(orow >= PAD, h + y, 0.0)


def _layer_call(h, sink, rc, rs, ng, win, qg, kg, cw, cb, bi, bf, og, wout, *, tb):
    batch, lp, d = h.shape
    const = lambda b, i: (0, 0)
    once = pl.Buffered(1)
    full = lambda a: pl.BlockSpec(a.shape, const)
    in_specs = [
        pl.BlockSpec(memory_space=pltpu.SMEM),
        pl.BlockSpec((1, tb, d), lambda b, i: (b, i, 0)),
        pl.BlockSpec((tb, LANES), lambda b, i: (i, 0)),
        pl.BlockSpec((tb, LANES), lambda b, i: (i, 0)),
        full(ng),
        pl.BlockSpec(win.shape, const, pipeline_mode=once),
        full(qg), full(kg), full(cw), full(cb), full(bi), full(bf), full(og),
        pl.BlockSpec(wout.shape, const, pipeline_mode=once),
    ]
    scratch = [
        pltpu.VMEM((tb, IN_COLS), F32),
        pltpu.VMEM((KV_HEADS, KEY_ROWS, LANES), BF16),
        pltpu.VMEM((KV_HEADS, KEY_ROWS, LANES), BF16),
        pltpu.VMEM((PAIRS_PER_KV * ATT_BLOCK, KEY_ROWS), BF16),
        pltpu.VMEM((tb + SUBLANES, QK_CONV_COLS), F32),
        pltpu.VMEM((tb, MIX_COLS), BF16),
        pltpu.VMEM((M_HEADS, M_QK_DIM, M_V_DIM), F32),
        pltpu.VMEM((M_HEADS, SUBLANES, LANES), F32),
        pltpu.VMEM((SUBLANES, LANES), F32),
    ]
    return pl.pallas_call(
        functools.partial(_layer_kernel, tb=tb),
        out_shape=jax.ShapeDtypeStruct(h.shape, h.dtype),
        grid=(batch, lp // tb),
        in_specs=in_specs,
        out_specs=pl.BlockSpec((1, tb, d), lambda b, i: (b, i, 0)),
        scratch_shapes=scratch,
        compiler_params=pltpu.CompilerParams(
            dimension_semantics=("arbitrary", "arbitrary"),
            vmem_limit_bytes=VMEM_LIMIT_BYTES),
        name="hybrid_layer",
    )(sink, h, rc, rs, ng, win, qg, kg, cw, cb, bi, bf, og, wout)


def _rope_tables(lp):
    pos = jnp.arange(lp, dtype=F32) - PAD
    inv_freq = ROPE_THETA ** (-jnp.arange(0, ROT_DIM, 2, dtype=F32) / ROT_DIM)
    ang = pos[:, None] * inv_freq[None, :]
    cos, sin = jnp.cos(ang), jnp.sin(ang)
    ones = jnp.ones((lp, HEAD_DIM - ROT_DIM), F32)
    zeros = jnp.zeros((lp, HEAD_DIM - ROT_DIM), F32)
    rc = jnp.concatenate([cos, cos, ones], axis=1)
    rs = jnp.concatenate([-sin, sin, zeros], axis=1)
    return jnp.tile(rc, (1, 2)), jnp.tile(rs, (1, 2))


def _pad_lanes(v):
    return jnp.pad(v.astype(F32), (0, LANES - v.shape[0]))[None, :]


def kernel(x, meta, norm_g, w_in, attn_q_norm_g, attn_k_norm_g, attn_sink, mlstm_conv_w,
           mlstm_conv_b, mlstm_b_i, mlstm_b_f, mlstm_out_norm_g, w_out):
    batch, seq, d = x.shape
    depth = w_in.shape[0]
    assert d == D_MODEL and seq % ROW_TILE == 0
    lp = seq + ATT_BLOCK
    head = jnp.concatenate([jnp.zeros((PAD, d), x.dtype), meta.astype(x.dtype)], axis=0)
    h = jnp.concatenate([jnp.broadcast_to(head[None], (batch, ATT_BLOCK, d)), x], axis=1)
    rc, rs = _rope_tables(lp)

    gate_lo = OFF_MO
    n_gate = 2 * M_HEADS
    for l in range(depth):
        w = w_in[l]
        gi = jnp.pad(w[:, gate_lo:gate_lo + M_HEADS], ((0, 0), (0, LANES - M_HEADS)))
        gf = jnp.pad(w[:, gate_lo + M_HEADS:gate_lo + n_gate], ((0, 0), (0, LANES - M_HEADS)))
        win = jnp.concatenate([w[:, :gate_lo], w[:, gate_lo + n_gate:], gi, gf], axis=1).astype(BF16)
        h = _layer_call(
            h, attn_sink[l].astype(F32), rc, rs,
            norm_g[l].astype(F32)[None, :], win,
            (jnp.tile(attn_q_norm_g[l].astype(F32), 2) * (HEAD_DIM ** -0.5))[None, :],
            jnp.tile(attn_k_norm_g[l].astype(F32), 2)[None, :],
            mlstm_conv_w[l].astype(F32), mlstm_conv_b[l].astype(F32)[None, :],
            _pad_lanes(mlstm_b_i[l]), _pad_lanes(mlstm_b_f[l]),
            mlstm_out_norm_g[l].astype(F32)[None, :], w_out[l].astype(BF16), tb=ROW_TILE)
    return h[:, ATT_BLOCK:]
```

```python
import functools

import jax
import jax.numpy as jnp
from jax import lax
from jax.experimental import pallas as pl
from jax.experimental.pallas import tpu as pltpu

D_MODEL = 1024
N_META = 16
HEAD_DIM = 64
ATT_HEADS = 16
KV_HEADS = 2
ATT_BLOCK = 128
ROT_DIM = 16
ROPE_THETA = 500000.0
M_HEADS = 4
M_V_DIM = 256
M_QK_DIM = 128
CHUNK = 64
CONV_K = 4
EPS = 1e-6
NEG = -1e30

LANES = 128
SUBLANES = 8
MXU_COLS = 256
PAD = ATT_BLOCK - N_META
HEAD_PAIRS = ATT_HEADS // 2
PAIRS_PER_KV = HEAD_PAIRS // KV_HEADS
KEY_ROWS = 5 * ATT_BLOCK
META_ROW = 4 * ATT_BLOCK
SINK_ROW = META_ROW + 2 * N_META

OFF_Q = 0
OFF_K = 1024
OFF_V = 1152
OFF_AG = 1280
OFF_MQ = 2304
OFF_MK = 2816
OFF_MV = 3328
OFF_MO = 4352
OFF_MG = 5376
OFF_GI = 6400
OFF_GF = 6528
IN_COLS = 6656
QK_CONV_COLS = 2 * M_HEADS * M_QK_DIM
MIX_COLS = 2 * D_MODEL

ROW_TILE = 128
IN_JOB_COLS = 2 * MXU_COLS
OUT_JOB_COLS = MXU_COLS
VMEM_LIMIT_BYTES = 56 * 1024 * 1024

F32 = jnp.float32
BF16 = jnp.bfloat16


def _sigmoid(x):
    return 1.0 / (1.0 + jnp.exp(-x))


def _silu(x):
    return x * _sigmoid(x)


def _log_sigmoid(x):
    return jnp.minimum(x, 0.0) - jnp.log1p(jnp.exp(-jnp.abs(x)))


def _dot(a, b):
    return jnp.dot(a, b, preferred_element_type=F32)


def _dot_nt(a, b):
    return lax.dot_general(a, b, (((1,), (1,)), ((), ())), preferred_element_type=F32)


def _dot_tn(a, b):
    return lax.dot_general(a, b, (((0,), (0,)), ((), ())), preferred_element_type=F32)


def _row_scan(x, row, combine, fill):
    shift = 1
    while shift < x.shape[0]:
        x = combine(x, jnp.where(row >= shift, pltpu.roll(x, shift, axis=0), fill))
        shift *= 2
    return x


def _interleave(a, b):
    out, ia, ib = [], 0, 0
    while ia < len(a) or ib < len(b):
        if ib >= len(b) or (ia < len(a) and ia * len(b) <= ib * len(a)):
            out.append(a[ia]); ia += 1
        else:
            out.append(b[ib]); ib += 1
    return out


def _layer_kernel(sink_ref, hn_ref, h_ref, rc_ref, rs_ref, ng_ref, win_ref, qg_ref, kg_ref, bd_ref,
                  cw_ref, cb_ref, bi_ref, bf_ref, og_ref, wout_ref, o_ref,
                  za_ref, zb_ref, ya_ref, yb_ref, hb_ref, kk_ref, vv_ref, pa_ref, pb_ref, u_ref,
                  c_ref, n_ref, m_ref, *, tb):
    s = pl.program_id(1)

    @pl.when(s == 0)
    def _():
        zb_ref[...] = jnp.zeros_like(zb_ref)
        yb_ref[...] = jnp.zeros_like(yb_ref)
        kk_ref[...] = jnp.zeros_like(kk_ref)
        r = lax.broadcasted_iota(jnp.int32, (KEY_ROWS, LANES), 0)
        l = lax.broadcasted_iota(jnp.int32, (KEY_ROWS, LANES), 1)
        top = (r < 2 * ATT_BLOCK) | ((r >= META_ROW) & (r < META_ROW + N_META)) | (r == SINK_ROW)
        bot = (((r >= 2 * ATT_BLOCK) & (r < META_ROW)) | ((r >= META_ROW + N_META) & (r < SINK_ROW))
               | (r == SINK_ROW + 1))
        ones = jnp.where(l < HEAD_DIM, jnp.where(top, 1.0, 0.0), jnp.where(bot, 1.0, 0.0)).astype(BF16)
        for c in range(KV_HEADS):
            vv_ref[c, :, 0:LANES] = jnp.zeros((KEY_ROWS, LANES), BF16)
            vv_ref[c, :, LANES:2 * LANES] = ones
        u_ref[0:SUBLANES, :] = jnp.zeros((SUBLANES, QK_CONV_COLS), F32)
        c_ref[...] = jnp.zeros_like(c_ref)
        n_ref[...] = jnp.zeros_like(n_ref)
        m_ref[...] = jnp.zeros_like(m_ref)

    step = functools.partial(
        _layer_step, sink_ref, hn_ref, h_ref, rc_ref, rs_ref, ng_ref, win_ref, qg_ref, kg_ref, bd_ref,
        cw_ref, cb_ref, bi_ref, bf_ref, og_ref, wout_ref, o_ref,
        hb_ref, kk_ref, vv_ref, (pa_ref, pb_ref), u_ref, c_ref, n_ref, m_ref, tb=tb)

    @pl.when(s % 2 == 0)
    def _():
        step(za_ref, zb_ref, ya_ref, yb_ref)

    @pl.when(s % 2 == 1)
    def _():
        step(zb_ref, za_ref, yb_ref, ya_ref)


def _layer_step(sink_ref, hn_ref, h_ref, rc_ref, rs_ref, ng_ref, win_ref, qg_ref, kg_ref, bd_ref,
                cw_ref, cb_ref, bi_ref, bf_ref, og_ref, wout_ref, o_ref,
                hb_ref, kk_ref, vv_ref, p_refs, u_ref, c_ref, n_ref, m_ref,
                zw_ref, z_ref, yw_ref, yr_ref, *, tb):
    s = pl.program_id(1)
    i = s - 1
    nblk = tb // ATT_BLOCK
    nchunk = tb // CHUNK

    hx = hn_ref[0]
    hb_ref[...] = (hx * lax.rsqrt(jnp.mean(hx * hx, axis=-1, keepdims=True) + EPS)
                   * ng_ref[...]).astype(BF16)
    orow = (s - 2) * tb + lax.broadcasted_iota(jnp.int32, (tb, 1), 0)

    def in_proj_job(c0):
        def run():
            zw_ref[:, c0:c0 + IN_JOB_COLS] = _dot(hb_ref[...], win_ref[:, c0:c0 + IN_JOB_COLS])
        return run

    def out_proj_job(c0):
        def run():
            y = _dot(yr_ref[...], wout_ref[:, c0:c0 + OUT_JOB_COLS])
            o_ref[0, :, c0:c0 + OUT_JOB_COLS] = jnp.where(
                orow >= PAD, h_ref[0, :, c0:c0 + OUT_JOB_COLS] + y, 0.0)
        return run

    jobs = _interleave([in_proj_job(c0) for c0 in range(0, IN_COLS, IN_JOB_COLS)],
                       [out_proj_job(c0) for c0 in range(0, D_MODEL, OUT_JOB_COLS)])

    lane = lax.broadcasted_iota(jnp.int32, (ATT_BLOCK, LANES), 1)
    lo_half = lane < HEAD_DIM
    rot_lo = (lane % HEAD_DIM) < (ROT_DIM // 2)

    def head_rms_inv(x):
        w = x.shape[1]
        sq = x * x
        hi = sq.astype(BF16)
        lo = (sq - hi.astype(F32)).astype(BF16)
        ones_bd = bd_ref[0:w, 0:w]
        ss = _dot(hi, ones_bd) + _dot(lo, ones_bd)
        return lax.rsqrt(ss * (1.0 / HEAD_DIM) + EPS)

    def rope(xn, rc, rs):
        partner = jnp.where(rot_lo, pltpu.roll(xn, LANES - ROT_DIM // 2, axis=1),
                            pltpu.roll(xn, ROT_DIM // 2, axis=1))
        return xn * rc + partner * rs

    attn_units, mlstm_units = [], []

    for j in range(nblk):
        r0 = j * ATT_BLOCK
        blk = i * nblk + j
        ctx = {}

        def kv_update(r0=r0, blk=blk, ctx=ctx):
            rc = rc_ref[r0:r0 + ATT_BLOCK, :]
            rs = rs_ref[r0:r0 + ATT_BLOCK, :]
            kx = z_ref[r0:r0 + ATT_BLOCK, OFF_K:OFF_K + LANES]
            kp = rope(kx * head_rms_inv(kx) * kg_ref[...], rc, rs)
            vp = z_ref[r0:r0 + ATT_BLOCK, OFF_V:OFF_V + LANES]
            kp_sw = pltpu.roll(kp, HEAD_DIM, axis=1)
            vp_sw = pltpu.roll(vp, HEAD_DIM, axis=1)
            for c in range(KV_HEADS):
                for ref, own, swapped in ((kk_ref, kp, kp_sw), (vv_ref, vp, vp_sw)):
                    ref[c, 0:ATT_BLOCK, 0:LANES] = ref[c, ATT_BLOCK:2 * ATT_BLOCK, 0:LANES]
                    ref[c, 2 * ATT_BLOCK:3 * ATT_BLOCK, 0:LANES] = ref[c, 3 * ATT_BLOCK:4 * ATT_BLOCK, 0:LANES]
                    top = jnp.where(lo_half, own if c == 0 else swapped, 0.0).astype(BF16)
                    bot = jnp.where(lo_half, 0.0, swapped if c == 0 else own).astype(BF16)
                    ref[c, ATT_BLOCK:2 * ATT_BLOCK, 0:LANES] = top
                    ref[c, 3 * ATT_BLOCK:4 * ATT_BLOCK, 0:LANES] = bot
                    for m0, part in ((META_ROW, top), (META_ROW + N_META, bot)):
                        keep = ref[c, m0:m0 + N_META, 0:LANES].astype(F32)
                        new = part[PAD:ATT_BLOCK, :].astype(F32)
                        ref[c, m0:m0 + N_META, 0:LANES] = jnp.where(blk == 0, new, keep).astype(BF16)

            qrow = blk * ATT_BLOCK + lax.broadcasted_iota(jnp.int32, (ATT_BLOCK, 2 * ATT_BLOCK), 0)
            krow = (blk - 1) * ATT_BLOCK + lax.broadcasted_iota(jnp.int32, (ATT_BLOCK, 2 * ATT_BLOCK), 1)
            dist = qrow - krow
            ctx["band_ok"] = (krow >= ATT_BLOCK) & (dist >= 0) & (dist < ATT_BLOCK)
            qrow_m = blk * ATT_BLOCK + lax.broadcasted_iota(jnp.int32, (ATT_BLOCK, LANES), 0)
            meta_ok = (PAD + (lane % N_META)) <= qrow_m
            ctx["meta_ok"] = ((lane < N_META) & meta_ok,
                              (lane >= N_META) & (lane < 2 * N_META) & meta_ok)

        def q_scores(c, r0=r0, ctx=ctx):
            rc = rc_ref[r0:r0 + ATT_BLOCK, :]
            rs = rs_ref[r0:r0 + ATT_BLOCK, :]
            gain2 = jnp.concatenate([qg_ref[...], qg_ref[...]], axis=1)
            q4 = []
            for half in range(PAIRS_PER_KV // 2):
                c0 = OFF_Q + (c * PAIRS_PER_KV + 2 * half) * LANES
                xq = z_ref[r0:r0 + ATT_BLOCK, c0:c0 + 2 * LANES]
                xn = xq * head_rms_inv(xq) * gain2
                q4.append(rope(xn[:, 0:LANES], rc, rs).astype(BF16))
                q4.append(rope(xn[:, LANES:2 * LANES], rc, rs).astype(BF16))
            ctx["s4", c] = _dot_nt(jnp.concatenate(q4, axis=0), kk_ref[c])

        def softmax(c, pi, ctx=ctx):
            p = c * PAIRS_PER_KV + pi
            sc = ctx["s4", c][pi * ATT_BLOCK:(pi + 1) * ATT_BLOCK, :]
            sm = sc[:, META_ROW:META_ROW + LANES]
            rows = slice(pi * ATT_BLOCK, (pi + 1) * ATT_BLOCK)
            pm = None
            for hh in range(2):
                sink = sink_ref[2 * p + hh]
                sb = jnp.where(ctx["band_ok"], sc[:, hh * 2 * ATT_BLOCK:(hh + 1) * 2 * ATT_BLOCK], NEG)
                smh = jnp.where(ctx["meta_ok"][hh], sm, NEG)
                both = jnp.maximum(jnp.maximum(sb[:, 0:LANES], sb[:, LANES:2 * LANES]), smh)
                mx = jnp.maximum(jnp.max(both, axis=-1, keepdims=True), sink)
                p_refs[c][rows, hh * 2 * ATT_BLOCK:(hh + 1) * 2 * ATT_BLOCK] = jnp.exp(sb - mx).astype(BF16)
                em = jnp.where(lane == 2 * N_META + hh, jnp.exp(sink - mx), jnp.exp(smh - mx))
                pm = em if pm is None else pm + em
            p_refs[c][rows, META_ROW:META_ROW + LANES] = pm.astype(BF16)

        def pv(c, r0=r0):
            o4 = _dot(p_refs[c][...], vv_ref[c])
            for pi in range(PAIRS_PER_KV):
                p = c * PAIRS_PER_KV + pi
                rows = slice(pi * ATT_BLOCK, (pi + 1) * ATT_BLOCK)
                gate = _silu(z_ref[r0:r0 + ATT_BLOCK, OFF_AG + p * LANES:OFF_AG + (p + 1) * LANES])
                yw_ref[r0:r0 + ATT_BLOCK, p * LANES:(p + 1) * LANES] = (
                    o4[rows, 0:LANES] * (1.0 / o4[rows, LANES:2 * LANES]) * gate).astype(BF16)

        attn_units.append(kv_update)
        for c in range(KV_HEADS):
            attn_units.append(functools.partial(q_scores, c))
            attn_units.extend(functools.partial(softmax, c, pi) for pi in range(PAIRS_PER_KV))
            attn_units.append(functools.partial(pv, c))

    crow = lax.broadcasted_iota(jnp.int32, (CHUNK, LANES), 0)
    tril = (lax.broadcasted_iota(jnp.int32, (CHUNK, CHUNK), 0)
            >= lax.broadcasted_iota(jnp.int32, (CHUNK, CHUNK), 1))
    for cj in range(nchunk):
        q0 = cj * CHUNK
        g = {}

        def chunk_prep(cj=cj, q0=q0, g=g):
            if cj == 0:
                u_ref[SUBLANES:SUBLANES + tb, :] = z_ref[:, OFF_MQ:OFF_MQ + QK_CONV_COLS]
            real = (i * tb + q0 + crow) >= PAD
            real_col = (i * tb + q0 + lax.broadcasted_iota(jnp.int32, (CHUNK, 1), 0)) >= PAD
            conv = cb_ref[...]
            for t in range(CONV_K):
                a0 = SUBLANES - (CONV_K - 1) + t + q0
                conv = conv + cw_ref[t:t + 1, :] * u_ref[a0:a0 + CHUNK, :]
            g["qk"] = jnp.where(real_col, _silu(conv), 0.0)
            if cj == nchunk - 1:
                u_ref[0:SUBLANES, :] = u_ref[tb:tb + SUBLANES, :]

            li = jnp.where(real, z_ref[q0:q0 + CHUNK, OFF_GI:OFF_GI + LANES] + bi_ref[...], NEG)
            lf = jnp.where(real, _log_sigmoid(z_ref[q0:q0 + CHUNK, OFF_GF:OFF_GF + LANES] + bf_ref[...]), 0.0)
            b = _row_scan(lf, crow, jnp.add, 0.0)
            r = li - b
            rmax = jnp.max(r, axis=0, keepdims=True)
            g["wg"] = jnp.exp(r - rmax)
            dmax = b + _row_scan(r, crow, jnp.maximum, NEG)
            m_in = m_ref[0:1, :]
            a_t = b + m_in
            m_t = jnp.maximum(a_t, dmax)
            g["inter"] = jnp.exp(a_t - m_t)
            g["cvec"] = b - m_t
            g["em_t"] = jnp.exp(-m_t)
            g["r_t"] = r.T
            b_last = b[CHUNK - 1:CHUNK, :]
            mg = b_last + rmax
            m_new = jnp.maximum(b_last + m_in, mg)
            g["a_st"] = jnp.exp(b_last + m_in - m_new)
            g["w_st"] = jnp.exp(mg - m_new)
            m_ref[...] = jnp.broadcast_to(m_new, (SUBLANES, LANES))

        def head(hd, q0=q0, g=g):
            qk = g["qk"]
            q = qk[:, hd * M_QK_DIM:(hd + 1) * M_QK_DIM]
            k = qk[:, (M_HEADS + hd) * M_QK_DIM:(M_HEADS + hd + 1) * M_QK_DIM] * (M_QK_DIM ** -0.5)
            vb = z_ref[q0:q0 + CHUNK, OFF_MV + hd * M_V_DIM:OFF_MV + (hd + 1) * M_V_DIM].astype(BF16)
            qb = q.astype(BF16)
            c_in = c_ref[hd]
            n_in = n_ref[hd, 0:1, :]

            arg = g["cvec"][:, hd:hd + 1] + g["r_t"][hd:hd + 1, :]
            sg = _dot_nt(qb, k.astype(BF16)) * jnp.exp(jnp.where(tril, arg, NEG))
            inter_c = g["inter"][:, hd:hd + 1]
            num = inter_c * _dot(qb, c_in.astype(BF16)) + _dot(sg.astype(BF16), vb)
            den = (inter_c * jnp.sum(q * n_in, axis=-1, keepdims=True)
                   + jnp.sum(sg, axis=-1, keepdims=True))
            hout = num / jnp.maximum(jnp.abs(den), g["em_t"][:, hd:hd + 1])

            kw = g["wg"][:, hd:hd + 1] * k
            a_s = g["a_st"][:, hd:hd + 1]
            w_s = g["w_st"][:, hd:hd + 1]
            c_ref[hd] = a_s * c_in + w_s * _dot_tn(kw.astype(BF16), vb)
            n_new = a_s * n_in + w_s * jnp.sum(kw, axis=0, keepdims=True)
            n_ref[hd] = jnp.broadcast_to(n_new, (SUBLANES, LANES))

            cols = slice(hd * M_V_DIM, (hd + 1) * M_V_DIM)
            hm = hout * lax.rsqrt(jnp.mean(hout * hout, axis=-1, keepdims=True) + EPS) * og_ref[:, cols]
            ocols = slice(OFF_MO + hd * M_V_DIM, OFF_MO + (hd + 1) * M_V_DIM)
            gcols = slice(OFF_MG + hd * M_V_DIM, OFF_MG + (hd + 1) * M_V_DIM)
            ym = hm * _sigmoid(z_ref[q0:q0 + CHUNK, ocols]) * _silu(z_ref[q0:q0 + CHUNK, gcols])
            yw_ref[q0:q0 + CHUNK, D_MODEL + hd * M_V_DIM:D_MODEL + (hd + 1) * M_V_DIM] = ym.astype(BF16)

        mlstm_units.append(chunk_prep)
        mlstm_units.extend(functools.partial(head, hd) for hd in range(M_HEADS))

    for unit in _interleave(_interleave(attn_units, mlstm_units), jobs):
        unit()


def _layer_call(h, sink, rc, rs, ng, win, qg, kg, bd, cw, cb, bi, bf, og, wout, *, tb):
    batch, lp, d = h.shape
    nt = lp // tb
    const = lambda b, s: (0, 0)
    proj_tile = lambda b, s: (b, jnp.minimum(s, nt - 1), 0)
    mix_rows = lambda b, s: (jnp.clip(s - 1, 0, nt - 1), 0)
    out_tile = lambda b, s: (b, jnp.maximum(s - 2, 0), 0)
    once = pl.Buffered(1)
    full = lambda a: pl.BlockSpec(a.shape, const)
    in_specs = [
        pl.BlockSpec(memory_space=pltpu.SMEM),
        pl.BlockSpec((1, tb, d), proj_tile),
        pl.BlockSpec((1, tb, d), out_tile),
        pl.BlockSpec((tb, LANES), mix_rows),
        pl.BlockSpec((tb, LANES), mix_rows),
        full(ng),
        pl.BlockSpec(win.shape, const, pipeline_mode=once),
        full(qg), full(kg), full(bd), full(cw), full(cb), full(bi), full(bf), full(og),
        pl.BlockSpec(wout.shape, const, pipeline_mode=once),
    ]
    scratch = [
        pltpu.VMEM((tb, IN_COLS), F32),
        pltpu.VMEM((tb, IN_COLS), F32),
        pltpu.VMEM((tb, MIX_COLS), BF16),
        pltpu.VMEM((tb, MIX_COLS), BF16),
        pltpu.VMEM((tb, D_MODEL), BF16),
        pltpu.VMEM((KV_HEADS, KEY_ROWS, LANES), BF16),
        pltpu.VMEM((KV_HEADS, KEY_ROWS, 2 * LANES), BF16),
        pltpu.VMEM((PAIRS_PER_KV * ATT_BLOCK, KEY_ROWS), BF16),
        pltpu.VMEM((PAIRS_PER_KV * ATT_BLOCK, KEY_ROWS), BF16),
        pltpu.VMEM((tb + SUBLANES, QK_CONV_COLS), F32),
        pltpu.VMEM((M_HEADS, M_QK_DIM, M_V_DIM), F32),
        pltpu.VMEM((M_HEADS, SUBLANES, LANES), F32),
        pltpu.VMEM((SUBLANES, LANES), F32),
    ]
    return pl.pallas_call(
        functools.partial(_layer_kernel, tb=tb),
        out_shape=jax.ShapeDtypeStruct(h.shape, h.dtype),
        grid=(batch, nt + 2),
        in_specs=in_specs,
        out_specs=pl.BlockSpec((1, tb, d), out_tile),
        scratch_shapes=scratch,
        compiler_params=pltpu.CompilerParams(
            dimension_semantics=("arbitrary", "arbitrary"),
            vmem_limit_bytes=VMEM_LIMIT_BYTES),
        name="hybrid_layer",
    )(sink, h, h, rc, rs, ng, win, qg, kg, bd, cw, cb, bi, bf, og, wout)


def _rope_tables(lp):
    pos = jnp.arange(lp, dtype=F32) - PAD
    inv_freq = ROPE_THETA ** (-jnp.arange(0, ROT_DIM, 2, dtype=F32) / ROT_DIM)
    ang = pos[:, None] * inv_freq[None, :]
    cos, sin = jnp.cos(ang), jnp.sin(ang)
    ones = jnp.ones((lp, HEAD_DIM - ROT_DIM), F32)
    zeros = jnp.zeros((lp, HEAD_DIM - ROT_DIM), F32)
    rc = jnp.concatenate([cos, cos, ones], axis=1)
    rs = jnp.concatenate([-sin, sin, zeros], axis=1)
    return jnp.tile(rc, (1, 2)), jnp.tile(rs, (1, 2))


def _pad_lanes(v):
    return jnp.pad(v.astype(F32), (0, LANES - v.shape[0]))[None, :]


def kernel(x, meta, norm_g, w_in, attn_q_norm_g, attn_k_norm_g, attn_sink, mlstm_conv_w,
           mlstm_conv_b, mlstm_b_i, mlstm_b_f, mlstm_out_norm_g, w_out):
    batch, seq, d = x.shape
    depth = w_in.shape[0]
    assert d == D_MODEL and seq % ROW_TILE == 0
    lp = seq + ATT_BLOCK
    head = jnp.concatenate([jnp.zeros((PAD, d), x.dtype), meta.astype(x.dtype)], axis=0)
    h = jnp.concatenate([jnp.broadcast_to(head[None], (batch, ATT_BLOCK, d)), x], axis=1)
    rc, rs = _rope_tables(lp)
    head_id = jnp.arange(2 * LANES) // HEAD_DIM
    ones_bd = (head_id[:, None] == head_id[None, :]).astype(BF16)

    gate_lo = OFF_MO
    n_gate = 2 * M_HEADS
    for l in range(depth):
        w = w_in[l]
        gi = jnp.pad(w[:, gate_lo:gate_lo + M_HEADS], ((0, 0), (0, LANES - M_HEADS)))
        gf = jnp.pad(w[:, gate_lo + M_HEADS:gate_lo + n_gate], ((0, 0), (0, LANES - M_HEADS)))
        win = jnp.concatenate([w[:, :gate_lo], w[:, gate_lo + n_gate:], gi, gf], axis=1).astype(BF16)
        h = _layer_call(
            h, attn_sink[l].astype(F32), rc, rs,
            norm_g[l].astype(F32)[None, :], win,
            (jnp.tile(attn_q_norm_g[l].astype(F32), 2) * (HEAD_DIM ** -0.5))[None, :],
            jnp.tile(attn_k_norm_g[l].astype(F32), 2)[None, :], ones_bd,
            mlstm_conv_w[l].astype(F32), mlstm_conv_b[l].astype(F32)[None, :],
            _pad_lanes(mlstm_b_i[l]), _pad_lanes(mlstm_b_f[l]),
            mlstm_out_norm_g[l].astype(F32)[None, :], w_out[l].astype(BF16), tb=ROW_TILE)
    return h[:, ATT_BLOCK:]
```

```python
import functools

import jax
import jax.numpy as jnp
from jax import lax
from jax.experimental import pallas as pl
from jax.experimental.pallas import tpu as pltpu

D_MODEL = 1024
N_META = 16
HEAD_DIM = 64
ATT_HEADS = 16
KV_HEADS = 2
ATT_BLOCK = 128
ROT_DIM = 16
ROPE_THETA = 500000.0
M_HEADS = 4
M_V_DIM = 256
M_QK_DIM = 128
CHUNK = 64
CONV_K = 4
EPS = 1e-6
NEG = -1e30
LOG2_E = 1.4426950408889634
NEG_LOG2_E = -LOG2_E

LANES = 128
SUBLANES = 8
MXU_COLS = 256
HEAD_PAIRS = ATT_HEADS // 2
PAIRS_PER_KV = HEAD_PAIRS // KV_HEADS
KEY_COLS = 5 * ATT_BLOCK
META_COL = 4 * ATT_BLOCK
SINK_LANE = 2 * N_META

ROW_TILE = 128
LEAD = ROW_TILE
PAD = LEAD - N_META
META_BLK = PAD // ATT_BLOCK
META_OFF = PAD % ATT_BLOCK
NBLK = ROW_TILE // ATT_BLOCK
BOT_ROW = (NBLK + 1) * ATT_BLOCK
META_ROW = 2 * BOT_ROW
KEY_ROWS = META_ROW + ATT_BLOCK

OFF_Q = 0
OFF_K = 1024
OFF_V = 1152
OFF_AG = 1280
OFF_MQ = 2304
OFF_MK = 2816
OFF_MV = 3328
OFF_MO = 4352
OFF_MG = 5376
OFF_GI = 6400
OFF_GF = 6528
IN_COLS = 6656
QK_CONV_COLS = 2 * M_HEADS * M_QK_DIM
MIX_COLS = 2 * D_MODEL

IN_JOB_COLS = MXU_COLS
OUT_JOB_COLS = MXU_COLS
TAIL_JOBS = 3
WOUT_PAD_COLS = LANES
VMEM_LIMIT_BYTES = 56 * 1024 * 1024

F32 = jnp.float32
BF16 = jnp.bfloat16


def _sigmoid(x):
    return 1.0 / (1.0 + jnp.exp2(x * NEG_LOG2_E))


def _silu(x):
    return x * _sigmoid(x)


def _log_sigmoid(x):
    return jnp.minimum(x, 0.0) - jnp.log1p(jnp.exp(-jnp.abs(x)))


def _dot(a, b):
    return jnp.dot(a, b, preferred_element_type=F32)


def _dot_nt(a, b):
    return lax.dot_general(a, b, (((1,), (1,)), ((), ())), preferred_element_type=F32)


def _dot_tn(a, b):
    return lax.dot_general(a, b, (((0,), (0,)), ((), ())), preferred_element_type=F32)


def _row_scan(x, row, combine, fill):
    shift = 1
    while shift < x.shape[0]:
        x = combine(x, jnp.where(row >= shift, pltpu.roll(x, shift, axis=0), fill))
        shift *= 2
    return x


def _interleave(a, b):
    out, ia, ib = [], 0, 0
    while ia < len(a) or ib < len(b):
        if ib >= len(b) or (ia < len(a) and ia * len(b) <= ib * len(a)):
            out.append(a[ia]); ia += 1
        else:
            out.append(b[ib]); ib += 1
    return out


def _layer_kernel(sink_ref, hn_ref, h_ref, rc_ref, rs_ref, ng_ref, win_ref, qg_ref, kg_ref, bd_ref,
                  cw_ref, cb_ref, bi_ref, bf_ref, og_ref, wout_ref, o_ref,
                  za_ref, zb_ref, ya_ref, yb_ref, hb_ref, kk_ref, vv_ref, p_ref, u_ref,
                  c_ref, n_ref, m_ref, *, nt):
    s = pl.program_id(0)

    @pl.when(s == 0)
    def _():
        zb_ref[...] = jnp.zeros_like(zb_ref)
        yb_ref[...] = jnp.zeros_like(yb_ref)

    @pl.when((s == 0) | (lax.rem(s - 1, nt) == 0))
    def _():
        kk_ref[...] = jnp.zeros_like(kk_ref)
        r = lax.broadcasted_iota(jnp.int32, (KEY_ROWS, LANES), 0)
        l = lax.broadcasted_iota(jnp.int32, (KEY_ROWS, LANES), 1)
        m = r - META_ROW
        top = (r < BOT_ROW) | ((m >= 0) & (m < N_META)) | (m == SINK_LANE)
        bot = ((r >= BOT_ROW) & (r < META_ROW)) | ((m >= N_META) & (m < SINK_LANE)) | (m == SINK_LANE + 1)
        ones = jnp.where(l < HEAD_DIM, jnp.where(top, 1.0, 0.0), jnp.where(bot, 1.0, 0.0)).astype(BF16)
        for c in range(KV_HEADS):
            vv_ref[c, :, 0:LANES] = jnp.zeros((KEY_ROWS, LANES), BF16)
            vv_ref[c, :, LANES:2 * LANES] = ones
        u_ref[ROW_TILE:ROW_TILE + SUBLANES, :] = jnp.zeros((SUBLANES, QK_CONV_COLS), F32)
        c_ref[...] = jnp.zeros_like(c_ref)
        n_ref[...] = jnp.zeros_like(n_ref)
        m_ref[...] = jnp.zeros_like(m_ref)

    step = functools.partial(
        _layer_step, sink_ref, hn_ref, h_ref, rc_ref, rs_ref, ng_ref, win_ref, qg_ref, kg_ref, bd_ref,
        cw_ref, cb_ref, bi_ref, bf_ref, og_ref, wout_ref, o_ref,
        hb_ref, kk_ref, vv_ref, p_ref, u_ref, c_ref, n_ref, m_ref, nt=nt)

    @pl.when(s % 2 == 0)
    def _():
        step(za_ref, zb_ref, ya_ref, yb_ref)

    @pl.when(s % 2 == 1)
    def _():
        step(zb_ref, za_ref, yb_ref, ya_ref)


def _layer_step(sink_ref, hn_ref, h_ref, rc_ref, rs_ref, ng_ref, win_ref, qg_ref, kg_ref, bd_ref,
                cw_ref, cb_ref, bi_ref, bf_ref, og_ref, wout_ref, o_ref,
                hb_ref, kk_ref, vv_ref, p_ref, u_ref, c_ref, n_ref, m_ref,
                zw_ref, z_ref, yw_ref, yr_ref, *, nt):
    tb = ROW_TILE
    s = pl.program_id(0)
    i = jnp.where(s < 1, -1, lax.rem(s - 1, nt))
    i_out = jnp.where(s < 2, -1, lax.rem(s - 2, nt))
    nchunk = tb // CHUNK

    hx = hn_ref[0]
    hb_ref[...] = (hx * lax.rsqrt(jnp.mean(hx * hx, axis=-1, keepdims=True) + EPS)
                   * ng_ref[...]).astype(BF16)
    orow = i_out * tb + lax.broadcasted_iota(jnp.int32, (tb, 1), 0)

    def in_proj_job(c0):
        def run():
            zw_ref[:, c0:c0 + IN_JOB_COLS] = _dot(hb_ref[...], win_ref[:, c0:c0 + IN_JOB_COLS])
        return run

    def out_proj_job(c0):
        def run():
            y = (_dot(yr_ref[:, 0:D_MODEL], wout_ref[0:D_MODEL, c0:c0 + OUT_JOB_COLS])
                 + _dot(yr_ref[:, D_MODEL:MIX_COLS], wout_ref[D_MODEL:MIX_COLS, c0:c0 + OUT_JOB_COLS]))
            o_ref[0, :, c0:c0 + OUT_JOB_COLS] = jnp.where(
                orow >= PAD, h_ref[0, :, c0:c0 + OUT_JOB_COLS] + y, 0.0)
        return run

    jobs = _interleave([in_proj_job(c0) for c0 in range(0, IN_COLS, IN_JOB_COLS)],
                       [out_proj_job(c0) for c0 in range(0, D_MODEL, OUT_JOB_COLS)])

    lane = lax.broadcasted_iota(jnp.int32, (ATT_BLOCK, LANES), 1)
    lo_half = lane < HEAD_DIM
    rot_lo = (lane % HEAD_DIM) < (ROT_DIM // 2)

    def head_rms_inv(x):
        w = x.shape[1]
        ss = _dot((x * x).astype(BF16), bd_ref[0:w, 0:w])
        return lax.rsqrt(ss * (1.0 / HEAD_DIM) + EPS)

    def rope(xn, rc, rs):
        partner = jnp.where(rot_lo, pltpu.roll(xn, LANES - ROT_DIM // 2, axis=1),
                            pltpu.roll(xn, ROT_DIM // 2, axis=1))
        return xn * rc + partner * rs

    def keys_of(ref, c, j):
        top = ref[c, j * ATT_BLOCK:(j + 2) * ATT_BLOCK, :]
        bot = ref[c, BOT_ROW + j * ATT_BLOCK:BOT_ROW + (j + 2) * ATT_BLOCK, :]
        return jnp.concatenate([top, bot, ref[c, META_ROW:META_ROW + ATT_BLOCK, :]], axis=0)

    attn_units, mlstm_units = [], []

    for j in range(NBLK):
        r0 = j * ATT_BLOCK
        blk = i * NBLK + j
        ctx = {}

        def kv_update(j=j, r0=r0, blk=blk, ctx=ctx):
            rc = rc_ref[r0:r0 + ATT_BLOCK, :]
            rs = rs_ref[r0:r0 + ATT_BLOCK, :]
            kx = z_ref[r0:r0 + ATT_BLOCK, OFF_K:OFF_K + LANES]
            kp = rope(kx * head_rms_inv(kx) * kg_ref[...], rc, rs)
            vp = z_ref[r0:r0 + ATT_BLOCK, OFF_V:OFF_V + LANES]
            kp_sw = pltpu.roll(kp, HEAD_DIM, axis=1)
            vp_sw = pltpu.roll(vp, HEAD_DIM, axis=1)
            slot = (j + 1) * ATT_BLOCK
            for c in range(KV_HEADS):
                for ref, own, swapped in ((kk_ref, kp, kp_sw), (vv_ref, vp, vp_sw)):
                    top = jnp.where(lo_half, own if c == 0 else swapped, 0.0).astype(BF16)
                    bot = jnp.where(lo_half, 0.0, swapped if c == 0 else own).astype(BF16)
                    ref[c, slot:slot + ATT_BLOCK, 0:LANES] = top
                    ref[c, BOT_ROW + slot:BOT_ROW + slot + ATT_BLOCK, 0:LANES] = bot
                    if j == META_BLK % NBLK:
                        for m0, part in ((META_ROW, top), (META_ROW + N_META, bot)):
                            keep = ref[c, m0:m0 + N_META, 0:LANES].astype(F32)
                            new = part[META_OFF:ATT_BLOCK, :].astype(F32)
                            ref[c, m0:m0 + N_META, 0:LANES] = jnp.where(blk == META_BLK, new, keep).astype(BF16)

            qrow = blk * ATT_BLOCK + lax.broadcasted_iota(jnp.int32, (ATT_BLOCK, 2 * ATT_BLOCK), 0)
            krow = (blk - 1) * ATT_BLOCK + lax.broadcasted_iota(jnp.int32, (ATT_BLOCK, 2 * ATT_BLOCK), 1)
            dist = qrow - krow
            ctx["band_ok"] = (krow >= LEAD) & (dist >= 0) & (dist < ATT_BLOCK)
            qrow_m = blk * ATT_BLOCK + lax.broadcasted_iota(jnp.int32, (ATT_BLOCK, LANES), 0)
            meta_ok = (PAD + (lane % N_META)) <= qrow_m
            ctx["meta_ok"] = ((lane < N_META) & meta_ok,
                              (lane >= N_META) & (lane < 2 * N_META) & meta_ok)
            ctx["meta_any"] = (lane < 2 * N_META) & meta_ok
            ctx["sink_lanes"] = (lane >> 1) == (SINK_LANE >> 1)
            ctx["first_head_lanes"] = (lane < N_META) | (lane == SINK_LANE)

        def q_scores(c, j=j, r0=r0, ctx=ctx):
            rc = rc_ref[r0:r0 + ATT_BLOCK, :]
            rs = rs_ref[r0:r0 + ATT_BLOCK, :]
            gain2 = jnp.concatenate([qg_ref[...], qg_ref[...]], axis=1)
            q4 = []
            for half in range(PAIRS_PER_KV // 2):
                c0 = OFF_Q + (c * PAIRS_PER_KV + 2 * half) * LANES
                xq = z_ref[r0:r0 + ATT_BLOCK, c0:c0 + 2 * LANES]
                xn = xq * head_rms_inv(xq) * gain2
                q4.append(rope(xn[:, 0:LANES], rc, rs).astype(BF16))
                q4.append(rope(xn[:, LANES:2 * LANES], rc, rs).astype(BF16))
            ctx["s4", c] = _dot_nt(jnp.concatenate(q4, axis=0), keys_of(kk_ref, c, j))

        def softmax(c, pi, j=j, ctx=ctx):
            p = c * PAIRS_PER_KV + pi
            pbuf = p_ref.at[j * KV_HEADS + c]
            sc = ctx["s4", c][pi * ATT_BLOCK:(pi + 1) * ATT_BLOCK, :]
            sm = sc[:, META_COL:META_COL + LANES]
            rows = slice(pi * ATT_BLOCK, (pi + 1) * ATT_BLOCK)
            mxs = []
            for hh in range(2):
                sink = sink_ref[2 * p + hh]
                sb = jnp.where(ctx["band_ok"], sc[:, hh * 2 * ATT_BLOCK:(hh + 1) * 2 * ATT_BLOCK], NEG)
                smh = jnp.where(ctx["meta_ok"][hh], sm, NEG)
                both = jnp.maximum(jnp.maximum(sb[:, 0:LANES], sb[:, LANES:2 * LANES]), smh)
                mx = jnp.maximum(jnp.max(both, axis=-1, keepdims=True), sink)
                mxs.append(mx)
                pbuf[rows, hh * 2 * ATT_BLOCK:(hh + 1) * 2 * ATT_BLOCK] = jnp.exp2(sb - mx).astype(BF16)
            sink_pair = jnp.where(lane == SINK_LANE, sink_ref[2 * p], sink_ref[2 * p + 1])
            sm2 = jnp.where(ctx["meta_any"], sm, jnp.where(ctx["sink_lanes"], sink_pair, NEG))
            mx2 = jnp.where(ctx["first_head_lanes"], mxs[0], mxs[1])
            pbuf[rows, META_COL:META_COL + LANES] = jnp.exp2(sm2 - mx2).astype(BF16)

        def pv(c, j=j, r0=r0):
            o4 = _dot(p_ref[j * KV_HEADS + c], keys_of(vv_ref, c, j))
            for pi in range(PAIRS_PER_KV):
                p = c * PAIRS_PER_KV + pi
                rows = slice(pi * ATT_BLOCK, (pi + 1) * ATT_BLOCK)
                gate = _silu(z_ref[r0:r0 + ATT_BLOCK, OFF_AG + p * LANES:OFF_AG + (p + 1) * LANES])
                yw_ref[r0:r0 + ATT_BLOCK, p * LANES:(p + 1) * LANES] = (
                    o4[rows, 0:LANES] * (1.0 / o4[rows, LANES:2 * LANES]) * gate).astype(BF16)

        attn_units.append((kv_update, 1))
        for c in range(KV_HEADS):
            attn_units.append((functools.partial(q_scores, c), 0))
            attn_units.extend((functools.partial(softmax, c, pi), 1) for pi in range(PAIRS_PER_KV))
            attn_units.append((functools.partial(pv, c), 0))

    def kv_carry():
        for ref in (kk_ref, vv_ref):
            for c in range(KV_HEADS):
                for base in (0, BOT_ROW):
                    ref[c, base:base + ATT_BLOCK, 0:LANES] = (
                        ref[c, base + NBLK * ATT_BLOCK:base + (NBLK + 1) * ATT_BLOCK, 0:LANES])

    crow = lax.broadcasted_iota(jnp.int32, (CHUNK, LANES), 0)
    tril = (lax.broadcasted_iota(jnp.int32, (CHUNK, CHUNK), 0)
            >= lax.broadcasted_iota(jnp.int32, (CHUNK, CHUNK), 1))
    for cj in range(nchunk):
        q0 = cj * CHUNK
        g = {}

        def chunk_prep(cj=cj, q0=q0, g=g):
            if cj == 0:
                u_ref[0:SUBLANES, :] = u_ref[tb:tb + SUBLANES, :]
                u_ref[SUBLANES:SUBLANES + tb, :] = z_ref[:, OFF_MQ:OFF_MQ + QK_CONV_COLS]
            real =(i * tb + q0 + crow) >= PAD
            real_col = (i * tb + q0 + lax.broadcasted_iota(jnp.int32, (CHUNK, 1), 0)) >= PAD
            uu = u_ref[q0:q0 + SUBLANES + CHUNK, :]
            acc = cw_ref[0:1, :] * uu
            for t in range(1, CONV_K):
                acc = cw_ref[t:t + 1, :] * uu + pltpu.roll(acc, 1, axis=0)
            conv = acc[SUBLANES:, :] + cb_ref[...]
            g["qk"] = jnp.where(real_col, _silu(conv), 0.0)

            li = jnp.where(real, z_ref[q0:q0 + CHUNK, OFF_GI:OFF_GI + LANES] + bi_ref[...], NEG)
            lf = jnp.where(real, _log_sigmoid(z_ref[q0:q0 + CHUNK, OFF_GF:OFF_GF + LANES] + bf_ref[...]), 0.0)
            b = _row_scan(lf, crow, jnp.add, 0.0)
            r = li - b
            rmax = jnp.max(r, axis=0, keepdims=True)
            g["wg"] = jnp.exp(r - rmax)
            dmax = b + _row_scan(r, crow, jnp.maximum, NEG)
            m_in = m_ref[0:1, :]
            a_t = b + m_in
            m_t = jnp.maximum(a_t, dmax)
            g["inter"] = jnp.exp(a_t - m_t)
            g["cvec"] = b - m_t
            g["em_t"] = jnp.exp(-m_t)
            g["r_t"] = r.T
            b_last = b[CHUNK - 1:CHUNK, :]
            mg = b_last + rmax
            m_new = jnp.maximum(b_last + m_in, mg)
            g["a_st"] = jnp.exp(b_last + m_in - m_new)
            g["w_st"] = jnp.exp(mg - m_new)
            m_ref[...] = jnp.broadcast_to(m_new, (SUBLANES, LANES))

        def head(hd, q0=q0, g=g):
            qk = g["qk"]
            q = qk[:, hd * M_QK_DIM:(hd + 1) * M_QK_DIM]
            k = qk[:, (M_HEADS + hd) * M_QK_DIM:(M_HEADS + hd + 1) * M_QK_DIM] * (M_QK_DIM ** -0.5)
            vb = z_ref[q0:q0 + CHUNK, OFF_MV + hd * M_V_DIM:OFF_MV + (hd + 1) * M_V_DIM].astype(BF16)
            qb = q.astype(BF16)
            c_in = c_ref[hd]
            n_in = n_ref[hd, 0:1, :]

            arg = g["cvec"][:, hd:hd + 1] + g["r_t"][hd:hd + 1, :]
            sg = _dot_nt(qb, k.astype(BF16)) * jnp.exp(jnp.where(tril, arg, NEG))
            inter_c = g["inter"][:, hd:hd + 1]
            num = inter_c * _dot(qb, c_in.astype(BF16)) + _dot(sg.astype(BF16), vb)
            den = (inter_c * jnp.sum(q * n_in, axis=-1, keepdims=True)
                   + jnp.sum(sg, axis=-1, keepdims=True))
            hout = num / jnp.maximum(jnp.abs(den), g["em_t"][:, hd:hd + 1])

            kw = g["wg"][:, hd:hd + 1] * k
            a_s = g["a_st"][:, hd:hd + 1]
            w_s = g["w_st"][:, hd:hd + 1]
            c_ref[hd] = a_s * c_in + w_s * _dot_tn(kw.astype(BF16), vb)
            n_new = a_s * n_in + w_s * jnp.sum(kw, axis=0, keepdims=True)
            n_ref[hd] = jnp.broadcast_to(n_new, (SUBLANES, LANES))

            cols = slice(hd * M_V_DIM, (hd + 1) * M_V_DIM)
            hm = hout * lax.rsqrt(jnp.mean(hout * hout, axis=-1, keepdims=True) + EPS) * og_ref[:, cols]
            ocols = slice(OFF_MO + hd * M_V_DIM, OFF_MO + (hd + 1) * M_V_DIM)
            gcols = slice(OFF_MG + hd * M_V_DIM, OFF_MG + (hd + 1) * M_V_DIM)
            ym = hm * _sigmoid(z_ref[q0:q0 + CHUNK, ocols]) * _silu(z_ref[q0:q0 + CHUNK, gcols])
            yw_ref[q0:q0 + CHUNK, D_MODEL + hd * M_V_DIM:D_MODEL + (hd + 1) * M_V_DIM] = ym.astype(BF16)

        mlstm_units.append((chunk_prep, 1))
        mlstm_units.extend((functools.partial(head, hd), 1) for hd in range(M_HEADS))

    kv_carry()
    units = _interleave(attn_units, mlstm_units)
    n_receivers = sum(takes for _, takes in units)
    n_spread = len(jobs) - TAIL_JOBS
    seen = 0
    for unit, takes in units:
        seen += takes
        while takes and len(jobs) > TAIL_JOBS and (n_spread + TAIL_JOBS - len(jobs)) * n_receivers < seen * n_spread:
            jobs.pop(0)()
        unit()
    for job in jobs:
        job()


def _layer_call(h, sink, rc, rs, ng, win, qg, kg, bd, cw, cb, bi, bf, og, wout):
    batch, lp, d = h.shape
    tb = ROW_TILE
    nt = lp // tb
    last = batch * nt - 1

    def tile_of(t):
        t = jnp.clip(t, 0, last)
        return t // nt, t % nt, 0

    const = lambda s: (0, 0)
    proj_tile = lambda s: tile_of(s)
    mix_rows = lambda s: tile_of(s - 1)[1:]
    out_tile = lambda s: tile_of(s - 2)
    once = pl.Buffered(1)
    full = lambda a: pl.BlockSpec(a.shape, const)
    in_specs = [
        pl.BlockSpec(memory_space=pltpu.SMEM),
        pl.BlockSpec((1, tb, d), proj_tile),
        pl.BlockSpec((1, tb, d), out_tile),
        pl.BlockSpec((tb, LANES), mix_rows),
        pl.BlockSpec((tb, LANES), mix_rows),
        full(ng),
        pl.BlockSpec(win.shape, const, pipeline_mode=once),
        full(qg), full(kg), full(bd), full(cw), full(cb), full(bi), full(bf), full(og),
        pl.BlockSpec(wout.shape, const, pipeline_mode=once),
    ]
    scratch = [
        pltpu.VMEM((tb, IN_COLS), F32),
        pltpu.VMEM((tb, IN_COLS), F32),
        pltpu.VMEM((tb, MIX_COLS), BF16),
        pltpu.VMEM((tb, MIX_COLS), BF16),
        pltpu.VMEM((tb, D_MODEL), BF16),
        pltpu.VMEM((KV_HEADS, KEY_ROWS, LANES), BF16),
        pltpu.VMEM((KV_HEADS, KEY_ROWS, 2 * LANES), BF16),
        pltpu.VMEM((NBLK * KV_HEADS, PAIRS_PER_KV * ATT_BLOCK, KEY_COLS), BF16),
        pltpu.VMEM((tb + SUBLANES, QK_CONV_COLS), F32),
        pltpu.VMEM((M_HEADS, M_QK_DIM, M_V_DIM), F32),
        pltpu.VMEM((M_HEADS, SUBLANES, LANES), F32),
        pltpu.VMEM((SUBLANES, LANES), F32),
    ]
    return pl.pallas_call(
        functools.partial(_layer_kernel, nt=nt),
        out_shape=jax.ShapeDtypeStruct(h.shape, h.dtype),
        grid=(batch * nt + 2,),
        in_specs=in_specs,
        out_specs=pl.BlockSpec((1, tb, d), out_tile),
        scratch_shapes=scratch,
        compiler_params=pltpu.CompilerParams(
            dimension_semantics=("arbitrary",),
            vmem_limit_bytes=VMEM_LIMIT_BYTES),
        name="hybrid_layer",
    )(sink, h, h, rc, rs, ng, win, qg, kg, bd, cw, cb, bi, bf, og, wout)


def _rope_tables(lp):
    pos = jnp.arange(lp, dtype=F32) - PAD
    inv_freq = ROPE_THETA ** (-jnp.arange(0, ROT_DIM, 2, dtype=F32) / ROT_DIM)
    ang = pos[:, None] * inv_freq[None, :]
    cos, sin = jnp.cos(ang), jnp.sin(ang)
    ones = jnp.ones((lp, HEAD_DIM - ROT_DIM), F32)
    zeros = jnp.zeros((lp, HEAD_DIM - ROT_DIM), F32)
    rc = jnp.concatenate([cos, cos, ones], axis=1)
    rs = jnp.concatenate([-sin, sin, zeros], axis=1)
    return jnp.tile(rc, (1, 2)), jnp.tile(rs, (1, 2))


def _pad_lanes(v):
    return jnp.pad(v.astype(F32), (0, LANES - v.shape[0]))[None, :]


def kernel(x, meta, norm_g, w_in, attn_q_norm_g, attn_k_norm_g, attn_sink, mlstm_conv_w,
           mlstm_conv_b, mlstm_b_i, mlstm_b_f, mlstm_out_norm_g, w_out):
    batch, seq, d = x.shape
    depth = w_in.shape[0]
    assert d == D_MODEL and seq % ROW_TILE == 0
    lp = seq + LEAD
    head = jnp.concatenate([jnp.zeros((PAD, d), x.dtype), meta.astype(x.dtype)], axis=0)
    h = jnp.concatenate([jnp.broadcast_to(head[None], (batch, LEAD, d)), x], axis=1)
    rc, rs = _rope_tables(lp)
    head_id = jnp.arange(2 * LANES) // HEAD_DIM
    ones_bd = (head_id[:, None] == head_id[None, :]).astype(BF16)

    gate_lo = OFF_MO
    n_gate = 2 * M_HEADS
    for l in range(depth):
        w = w_in[l]
        gi = jnp.pad(w[:, gate_lo:gate_lo + M_HEADS], ((0, 0), (0, LANES - M_HEADS)))
        gf = jnp.pad(w[:, gate_lo + M_HEADS:gate_lo + n_gate], ((0, 0), (0, LANES - M_HEADS)))
        win = jnp.concatenate([w[:, :gate_lo], w[:, gate_lo + n_gate:], gi, gf], axis=1).astype(BF16)
        h = _layer_call(
            h, attn_sink[l].astype(F32) * LOG2_E, rc, rs,
            norm_g[l].astype(F32)[None, :], win,
            (jnp.tile(attn_q_norm_g[l].astype(F32), 2) * (HEAD_DIM ** -0.5 * LOG2_E))[None, :],
            jnp.tile(attn_k_norm_g[l].astype(F32), 2)[None, :], ones_bd,
            mlstm_conv_w[l].astype(F32), mlstm_conv_b[l].astype(F32)[None, :],
            _pad_lanes(mlstm_b_i[l]), _pad_lanes(mlstm_b_f[l]),
            mlstm_out_norm_g[l].astype(F32)[None, :],
            jnp.pad(w_out[l].astype(BF16), ((0, 0), (0, WOUT_PAD_COLS))))
    return h[:, LEAD:]
```

```python
import functools

import jax
import jax.numpy as jnp
from jax import lax
from jax.experimental import pallas as pl
from jax.experimental.pallas import tpu as pltpu

D_MODEL = 1024
N_META = 16
HEAD_DIM = 64
ATT_HEADS = 16
KV_HEADS = 2
ATT_BLOCK = 128
ROT_DIM = 16
ROPE_THETA = 500000.0
M_HEADS = 4
M_V_DIM = 256
M_QK_DIM = 128
CHUNK = 64
CONV_K = 4
EPS = 1e-6
NEG = -1e30
LOG2_E = 1.4426950408889634
NEG_LOG2_E = -LOG2_E

LANES = 128
SUBLANES = 8
MXU_COLS = 256
HEAD_PAIRS = ATT_HEADS // 2
PAIRS_PER_KV = HEAD_PAIRS // KV_HEADS
KEY_COLS = 5 * ATT_BLOCK
META_COL = 4 * ATT_BLOCK
SINK_LANE = 2 * N_META

ROW_TILE = 128
LEAD = ROW_TILE
PAD = LEAD - N_META
META_BLK = PAD // ATT_BLOCK
META_OFF = PAD % ATT_BLOCK
NBLK = ROW_TILE // ATT_BLOCK
BOT_ROW = (NBLK + 1) * ATT_BLOCK
META_ROW = 2 * BOT_ROW
KEY_ROWS = META_ROW + ATT_BLOCK

OFF_Q = 0
OFF_K = 1024
OFF_V = 1152
OFF_AG = 1280
OFF_MQ = 2304
OFF_MK = 2816
OFF_MV = 3328
OFF_MO = 4352
OFF_MG = 5376
OFF_GI = 6400
OFF_GF = 6528
IN_COLS = 6656
QK_CONV_COLS = 2 * M_HEADS * M_QK_DIM
MIX_COLS = 2 * D_MODEL

IN_JOB_COLS = MXU_COLS
OUT_JOB_COLS = MXU_COLS
TAIL_JOBS = 3
WOUT_PAD_COLS = LANES
VMEM_LIMIT_BYTES = 56 * 1024 * 1024

F32 = jnp.float32
BF16 = jnp.bfloat16


def _sigmoid(x):
    return 1.0 / (1.0 + jnp.exp2(x * NEG_LOG2_E))


def _silu(x):
    return x * _sigmoid(x)


def _log_sigmoid(x):
    return jnp.minimum(x, 0.0) - jnp.log1p(jnp.exp(-jnp.abs(x)))


def _dot(a, b):
    return jnp.dot(a, b, preferred_element_type=F32)


def _dot_nt(a, b):
    return lax.dot_general(a, b, (((1,), (1,)), ((), ())), preferred_element_type=F32)


def _dot_tn(a, b):
    return lax.dot_general(a, b, (((0,), (0,)), ((), ())), preferred_element_type=F32)


def _row_scan(x, row, combine, fill):
    shift = 1
    while shift < x.shape[0]:
        x = combine(x, jnp.where(row >= shift, pltpu.roll(x, shift, axis=0), fill))
        shift *= 2
    return x


def _interleave(a, b):
    out, ia, ib = [], 0, 0
    while ia < len(a) or ib < len(b):
        if ib >= len(b) or (ia < len(a) and ia * len(b) <= ib * len(a)):
            out.append(a[ia]); ia += 1
        else:
            out.append(b[ib]); ib += 1
    return out


def _layer_kernel(sink_ref, hn_ref, h_ref, head_ref, rc_ref, rs_ref, ng_ref, win_ref, qg_ref, kg_ref, bd_ref,
                  cw_ref, cb_ref, bi_ref, bf_ref, og_ref, wout_ref, o_ref,
                  za_ref, zb_ref, ya_ref, yb_ref, hb_ref, kk_ref, vv_ref, p_ref, u_ref,
                  c_ref, n_ref, m_ref, *, nt, n_tiles, first):
    s = pl.program_id(0)

    @pl.when(s == 0)
    def _():
        zb_ref[...] = jnp.zeros_like(zb_ref)
        yb_ref[...] = jnp.zeros_like(yb_ref)

    @pl.when((s == 0) | (lax.rem(s - 1, nt) == 0))
    def _():
        kk_ref[...] = jnp.zeros_like(kk_ref)
        r = lax.broadcasted_iota(jnp.int32, (KEY_ROWS, LANES), 0)
        l = lax.broadcasted_iota(jnp.int32, (KEY_ROWS, LANES), 1)
        m = r - META_ROW
        top = (r < BOT_ROW) | ((m >= 0) & (m < N_META)) | (m == SINK_LANE)
        bot = ((r >= BOT_ROW) & (r < META_ROW)) | ((m >= N_META) & (m < SINK_LANE)) | (m == SINK_LANE + 1)
        ones = jnp.where(l < HEAD_DIM, jnp.where(top, 1.0, 0.0), jnp.where(bot, 1.0, 0.0)).astype(BF16)
        for c in range(KV_HEADS):
            vv_ref[c, :, 0:LANES] = jnp.zeros((KEY_ROWS, LANES), BF16)
            vv_ref[c, :, LANES:2 * LANES] = ones
        u_ref[ROW_TILE:ROW_TILE + SUBLANES, :] = jnp.zeros((SUBLANES, QK_CONV_COLS), F32)
        c_ref[...] = jnp.zeros_like(c_ref)
        n_ref[...] = jnp.zeros_like(n_ref)
        m_ref[...] = jnp.zeros_like(m_ref)

    step = functools.partial(
        _layer_step, sink_ref, hn_ref, h_ref, head_ref, rc_ref, rs_ref, ng_ref, win_ref, qg_ref, kg_ref, bd_ref,
        cw_ref, cb_ref, bi_ref, bf_ref, og_ref, wout_ref, o_ref,
        hb_ref, kk_ref, vv_ref, p_ref, u_ref, c_ref, n_ref, m_ref, nt=nt, n_tiles=n_tiles, first=first)

    @pl.when(s % 2 == 0)
    def _():
        step(za_ref, zb_ref, ya_ref, yb_ref)

    @pl.when(s % 2 == 1)
    def _():
        step(zb_ref, za_ref, yb_ref, ya_ref)


def _layer_step(sink_ref, hn_ref, h_ref, head_ref, rc_ref, rs_ref, ng_ref, win_ref, qg_ref, kg_ref, bd_ref,
                cw_ref, cb_ref, bi_ref, bf_ref, og_ref, wout_ref, o_ref,
                hb_ref, kk_ref, vv_ref, p_ref, u_ref, c_ref, n_ref, m_ref,
                zw_ref, z_ref, yw_ref, yr_ref, *, nt, n_tiles, first):
    tb = ROW_TILE
    s = pl.program_id(0)
    i = jnp.where(s < 1, -1, lax.rem(s - 1, nt))
    i_out = jnp.where(s < 2, -1, lax.rem(s - 2, nt))
    nchunk = tb // CHUNK

    hx = hn_ref[0]
    if first:
        hx = jnp.where(lax.rem(jnp.minimum(s, n_tiles - 1), nt) == 0, head_ref[...], hx)
    hb_ref[...] = (hx * lax.rsqrt(jnp.mean(hx * hx, axis=-1, keepdims=True) + EPS)
                   * ng_ref[...]).astype(BF16)
    orow = i_out * tb + lax.broadcasted_iota(jnp.int32, (tb, 1), 0)

    def in_proj_job(c0):
        def run():
            zw_ref[:, c0:c0 + IN_JOB_COLS] = _dot(hb_ref[...], win_ref[:, c0:c0 + IN_JOB_COLS])
        return run

    def out_proj_job(c0):
        def run():
            y = (_dot(yr_ref[:, 0:D_MODEL], wout_ref[0:D_MODEL, c0:c0 + OUT_JOB_COLS])
                 + _dot(yr_ref[:, D_MODEL:MIX_COLS], wout_ref[D_MODEL:MIX_COLS, c0:c0 + OUT_JOB_COLS]))
            res = h_ref[0, :, c0:c0 + OUT_JOB_COLS]
            if first:
                res = jnp.where(i_out == 0, head_ref[:, c0:c0 + OUT_JOB_COLS], res)
            o_ref[0, :, c0:c0 + OUT_JOB_COLS] = jnp.where(orow >= PAD, res + y, 0.0)
        return run

    jobs = _interleave([in_proj_job(c0) for c0 in range(0, IN_COLS, IN_JOB_COLS)],
                       [out_proj_job(c0) for c0 in range(0, D_MODEL, OUT_JOB_COLS)])

    lane = lax.broadcasted_iota(jnp.int32, (ATT_BLOCK, LANES), 1)
    lo_half = lane < HEAD_DIM
    rot_lo = (lane % HEAD_DIM) < (ROT_DIM // 2)

    def head_rms_inv(x):
        w = x.shape[1]
        ss = _dot((x * x).astype(BF16), bd_ref[0:w, 0:w])
        return lax.rsqrt(ss * (1.0 / HEAD_DIM) + EPS)

    def rope(xn, rc, rs):
        partner = jnp.where(rot_lo, pltpu.roll(xn, LANES - ROT_DIM // 2, axis=1),
                            pltpu.roll(xn, ROT_DIM // 2, axis=1))
        return xn * rc + partner * rs

    def keys_of(ref, c, j):
        top = ref[c, j * ATT_BLOCK:(j + 2) * ATT_BLOCK, :]
        bot = ref[c, BOT_ROW + j * ATT_BLOCK:BOT_ROW + (j + 2) * ATT_BLOCK, :]
        return jnp.concatenate([top, bot, ref[c, META_ROW:META_ROW + ATT_BLOCK, :]], axis=0)

    attn_units, mlstm_units = [], []

    for j in range(NBLK):
        r0 = j * ATT_BLOCK
        blk = i * NBLK + j
        ctx = {}

        def kv_update(j=j, r0=r0, blk=blk, ctx=ctx):
            rc = rc_ref[r0:r0 + ATT_BLOCK, :]
            rs = rs_ref[r0:r0 + ATT_BLOCK, :]
            kx = z_ref[r0:r0 + ATT_BLOCK, OFF_K:OFF_K + LANES]
            kp = rope(kx * head_rms_inv(kx) * kg_ref[...], rc, rs)
            vp = z_ref[r0:r0 + ATT_BLOCK, OFF_V:OFF_V + LANES]
            kp_sw = pltpu.roll(kp, HEAD_DIM, axis=1)
            vp_sw = pltpu.roll(vp, HEAD_DIM, axis=1)
            slot = (j + 1) * ATT_BLOCK
            for c in range(KV_HEADS):
                for ref, own, swapped in ((kk_ref, kp, kp_sw), (vv_ref, vp, vp_sw)):
                    top = jnp.where(lo_half, own if c == 0 else swapped, 0.0).astype(BF16)
                    bot = jnp.where(lo_half, 0.0, swapped if c == 0 else own).astype(BF16)
                    ref[c, slot:slot + ATT_BLOCK, 0:LANES] = top
                    ref[c, BOT_ROW + slot:BOT_ROW + slot + ATT_BLOCK, 0:LANES] = bot
                    if j == META_BLK % NBLK:
                        for m0, part in ((META_ROW, top), (META_ROW + N_META, bot)):
                            keep = ref[c, m0:m0 + N_META, 0:LANES].astype(F32)
                            new = part[META_OFF:ATT_BLOCK, :].astype(F32)
                            ref[c, m0:m0 + N_META, 0:LANES] = jnp.where(blk == META_BLK, new, keep).astype(BF16)

            qrow = blk * ATT_BLOCK + lax.broadcasted_iota(jnp.int32, (ATT_BLOCK, 2 * ATT_BLOCK), 0)
            krow = (blk - 1) * ATT_BLOCK + lax.broadcasted_iota(jnp.int32, (ATT_BLOCK, 2 * ATT_BLOCK), 1)
            dist = qrow - krow
            ctx["band_ok"] = (krow >= LEAD) & (dist >= 0) & (dist < ATT_BLOCK)
            qrow_m = blk * ATT_BLOCK + lax.broadcasted_iota(jnp.int32, (ATT_BLOCK, LANES), 0)
            meta_ok = (PAD + (lane % N_META)) <= qrow_m
            ctx["meta_ok"] = ((lane < N_META) & meta_ok,
                              (lane >= N_META) & (lane < 2 * N_META) & meta_ok)
            ctx["meta_any"] = (lane < 2 * N_META) & meta_ok
            ctx["sink_lanes"] = (lane >> 1) == (SINK_LANE >> 1)
            ctx["first_head_lanes"] = (lane < N_META) | (lane == SINK_LANE)

        def q_scores(c, j=j, r0=r0, ctx=ctx):
            rc = rc_ref[r0:r0 + ATT_BLOCK, :]
            rs = rs_ref[r0:r0 + ATT_BLOCK, :]
            gain2 = jnp.concatenate([qg_ref[...], qg_ref[...]], axis=1)
            q4 = []
            for half in range(PAIRS_PER_KV // 2):
                c0 = OFF_Q + (c * PAIRS_PER_KV + 2 * half) * LANES
                xq = z_ref[r0:r0 + ATT_BLOCK, c0:c0 + 2 * LANES]
                xn = xq * head_rms_inv(xq) * gain2
                q4.append(rope(xn[:, 0:LANES], rc, rs).astype(BF16))
                q4.append(rope(xn[:, LANES:2 * LANES], rc, rs).astype(BF16))
            ctx["s4", c] = _dot_nt(jnp.concatenate(q4, axis=0), keys_of(kk_ref, c, j))

        def softmax(c, pi, j=j, ctx=ctx):
            p = c * PAIRS_PER_KV + pi
            pbuf = p_ref.at[j * KV_HEADS + c]
            sc = ctx["s4", c][pi * ATT_BLOCK:(pi + 1) * ATT_BLOCK, :]
            sm = sc[:, META_COL:META_COL + LANES]
            rows = slice(pi * ATT_BLOCK, (pi + 1) * ATT_BLOCK)
            mxs = []
            for hh in range(2):
                sink = sink_ref[2 * p + hh]
                sb = jnp.where(ctx["band_ok"], sc[:, hh * 2 * ATT_BLOCK:(hh + 1) * 2 * ATT_BLOCK], NEG)
                smh = jnp.where(ctx["meta_ok"][hh], sm, NEG)
                both = jnp.maximum(jnp.maximum(sb[:, 0:LANES], sb[:, LANES:2 * LANES]), smh)
                mx = jnp.maximum(jnp.max(both, axis=-1, keepdims=True), sink)
                mxs.append(mx)
                pbuf[rows, hh * 2 * ATT_BLOCK:(hh + 1) * 2 * ATT_BLOCK] = jnp.exp2(sb - mx).astype(BF16)
            sink_pair = jnp.where(lane == SINK_LANE, sink_ref[2 * p], sink_ref[2 * p + 1])
            sm2 = jnp.where(ctx["meta_any"], sm, jnp.where(ctx["sink_lanes"], sink_pair, NEG))
            mx2 = jnp.where(ctx["first_head_lanes"], mxs[0], mxs[1])
            pbuf[rows, META_COL:META_COL + LANES] = jnp.exp2(sm2 - mx2).astype(BF16)

        def pv(c, j=j, r0=r0):
            o4 = _dot(p_ref[j * KV_HEADS + c], keys_of(vv_ref, c, j))
            for pi in range(PAIRS_PER_KV):
                p = c * PAIRS_PER_KV + pi
                rows = slice(pi * ATT_BLOCK, (pi + 1) * ATT_BLOCK)
                gate = _silu(z_ref[r0:r0 + ATT_BLOCK, OFF_AG + p * LANES:OFF_AG + (p + 1) * LANES])
                yw_ref[r0:r0 + ATT_BLOCK, p * LANES:(p + 1) * LANES] = (
                    o4[rows, 0:LANES] * (1.0 / o4[rows, LANES:2 * LANES]) * gate).astype(BF16)

        attn_units.append((kv_update, 1))
        for c in range(KV_HEADS):
            attn_units.append((functools.partial(q_scores, c), 0))
            attn_units.extend((functools.partial(softmax, c, pi), 1) for pi in range(PAIRS_PER_KV))
            attn_units.append((functools.partial(pv, c), 0))

    def kv_carry():
        for ref in (kk_ref, vv_ref):
            for c in range(KV_HEADS):
                for base in (0, BOT_ROW):
                    ref[c, base:base + ATT_BLOCK, 0:LANES] = (
                        ref[c, base + NBLK * ATT_BLOCK:base + (NBLK + 1) * ATT_BLOCK, 0:LANES])

    crow = lax.broadcasted_iota(jnp.int32, (CHUNK, LANES), 0)
    tril = (lax.broadcasted_iota(jnp.int32, (CHUNK, CHUNK), 0)
            >= lax.broadcasted_iota(jnp.int32, (CHUNK, CHUNK), 1))
    for cj in range(nchunk):
        q0 = cj * CHUNK
        g = {}

        def chunk_prep(cj=cj, q0=q0, g=g):
            if cj == 0:
                u_ref[0:SUBLANES, :] = u_ref[tb:tb + SUBLANES, :]
                u_ref[SUBLANES:SUBLANES + tb, :] = z_ref[:, OFF_MQ:OFF_MQ + QK_CONV_COLS]
            real =(i * tb + q0 + crow) >= PAD
            real_col = (i * tb + q0 + lax.broadcasted_iota(jnp.int32, (CHUNK, 1), 0)) >= PAD
            uu = u_ref[q0:q0 + SUBLANES + CHUNK, :]
            acc = cw_ref[0:1, :] * uu
            for t in range(1, CONV_K):
                acc = cw_ref[t:t + 1, :] * uu + pltpu.roll(acc, 1, axis=0)
            conv = acc[SUBLANES:, :] + cb_ref[...]
            g["qk"] = jnp.where(real_col, _silu(conv), 0.0)

            li = jnp.where(real, z_ref[q0:q0 + CHUNK, OFF_GI:OFF_GI + LANES] + bi_ref[...], NEG)
            lf = jnp.where(real, _log_sigmoid(z_ref[q0:q0 + CHUNK, OFF_GF:OFF_GF + LANES] + bf_ref[...]), 0.0)
            b = _row_scan(lf, crow, jnp.add, 0.0)
            r = li - b
            rmax = jnp.max(r, axis=0, keepdims=True)
            g["wg"] = jnp.exp(r - rmax)
            dmax = b + _row_scan(r, crow, jnp.maximum, NEG)
            m_in = m_ref[0:1, :]
            a_t = b + m_in
            m_t = jnp.maximum(a_t, dmax)
            g["inter"] = jnp.exp(a_t - m_t)
            g["cvec"] = b - m_t
            g["em_t"] = jnp.exp(-m_t)
            g["r_t"] = r.T
            b_last = b[CHUNK - 1:CHUNK, :]
            mg = b_last + rmax
            m_new = jnp.maximum(b_last + m_in, mg)
            g["a_st"] = jnp.exp(b_last + m_in - m_new)
            g["w_st"] = jnp.exp(mg - m_new)
            m_ref[...] = jnp.broadcast_to(m_new, (SUBLANES, LANES))

        def head(hd, q0=q0, g=g):
            qk = g["qk"]
            q = qk[:, hd * M_QK_DIM:(hd + 1) * M_QK_DIM]
            k = qk[:, (M_HEADS + hd) * M_QK_DIM:(M_HEADS + hd + 1) * M_QK_DIM] * (M_QK_DIM ** -0.5)
            vb = z_ref[q0:q0 + CHUNK, OFF_MV + hd * M_V_DIM:OFF_MV + (hd + 1) * M_V_DIM].astype(BF16)
            qb = q.astype(BF16)
            c_in = c_ref[hd]
            n_in = n_ref[hd, 0:1, :]

            arg = g["cvec"][:, hd:hd + 1] + g["r_t"][hd:hd + 1, :]
            sg = _dot_nt(qb, k.astype(BF16)) * jnp.exp(jnp.where(tril, arg, NEG))
            inter_c = g["inter"][:, hd:hd + 1]
            num = inter_c * _dot(qb, c_in.astype(BF16)) + _dot(sg.astype(BF16), vb)
            den = (inter_c * jnp.sum(q * n_in, axis=-1, keepdims=True)
                   + jnp.sum(sg, axis=-1, keepdims=True))
            hout = num / jnp.maximum(jnp.abs(den), g["em_t"][:, hd:hd + 1])

            kw = g["wg"][:, hd:hd + 1] * k
            a_s = g["a_st"][:, hd:hd + 1]
            w_s = g["w_st"][:, hd:hd + 1]
            c_ref[hd] = a_s * c_in + w_s * _dot_tn(kw.astype(BF16), vb)
            n_new = a_s * n_in + w_s * jnp.sum(kw, axis=0, keepdims=True)
            n_ref[hd] = jnp.broadcast_to(n_new, (SUBLANES, LANES))

            cols = slice(hd * M_V_DIM, (hd + 1) * M_V_DIM)
            hm = hout * lax.rsqrt(jnp.mean(hout * hout, axis=-1, keepdims=True) + EPS) * og_ref[:, cols]
            ocols = slice(OFF_MO + hd * M_V_DIM, OFF_MO + (hd + 1) * M_V_DIM)
            gcols = slice(OFF_MG + hd * M_V_DIM, OFF_MG + (hd + 1) * M_V_DIM)
            ym = hm * _sigmoid(z_ref[q0:q0 + CHUNK, ocols]) * _silu(z_ref[q0:q0 + CHUNK, gcols])
            yw_ref[q0:q0 + CHUNK, D_MODEL + hd * M_V_DIM:D_MODEL + (hd + 1) * M_V_DIM] = ym.astype(BF16)

        mlstm_units.append((chunk_prep, 1))
        mlstm_units.extend((functools.partial(head, hd), 1) for hd in range(M_HEADS))

    kv_carry()
    units = _interleave(attn_units, mlstm_units)
    n_receivers = sum(takes for _, takes in units)
    n_spread = len(jobs) - TAIL_JOBS
    seen = 0
    for unit, takes in units:
        seen += takes
        while takes and len(jobs) > TAIL_JOBS and (n_spread + TAIL_JOBS - len(jobs)) * n_receivers < seen * n_spread:
            jobs.pop(0)()
        unit()
    for job in jobs:
        job()


def _layer_call(h, head, sink, rc, rs, ng, win, qg, kg, bd, cw, cb, bi, bf, og, wout, *, layer, first, last):
    batch, rows, d = h.shape
    tb = ROW_TILE
    nt = (rows + LEAD if first else rows) // tb
    last_tile = batch * nt - 1

    def tile_of(t):
        t = jnp.clip(t, 0, last_tile)
        return t // nt, t % nt, 0

    def token_tile_of(t):
        b, i, _ = tile_of(t)
        return b, jnp.maximum(i - LEAD // tb, 0), 0

    const = lambda s: (0, 0)
    in_tile = token_tile_of if first else tile_of
    mix_rows = lambda s: tile_of(s - 1)[1:]
    out_tile = token_tile_of if last else tile_of
    out_rows = rows + (LEAD if first else 0) - (LEAD if last else 0)
    once = pl.Buffered(1)
    full = lambda a: pl.BlockSpec(a.shape, const)
    in_specs = [
        pl.BlockSpec(memory_space=pltpu.SMEM),
        pl.BlockSpec((1, tb, d), lambda s: in_tile(s)),
        pl.BlockSpec((1, tb, d), lambda s: in_tile(s - 2)),
        full(head),
        pl.BlockSpec((tb, LANES), mix_rows),
        pl.BlockSpec((tb, LANES), mix_rows),
        full(ng),
        pl.BlockSpec((None,) + win.shape[1:], lambda s: (layer, 0, 0), pipeline_mode=once),
        full(qg), full(kg), full(bd), full(cw), full(cb), full(bi), full(bf), full(og),
        pl.BlockSpec((None,) + wout.shape[1:], lambda s: (layer, 0, 0), pipeline_mode=once),
    ]
    scratch = [
        pltpu.VMEM((tb, IN_COLS), F32),
        pltpu.VMEM((tb, IN_COLS), F32),
        pltpu.VMEM((tb, MIX_COLS), BF16),
        pltpu.VMEM((tb, MIX_COLS), BF16),
        pltpu.VMEM((tb, D_MODEL), BF16),
        pltpu.VMEM((KV_HEADS, KEY_ROWS, LANES), BF16),
        pltpu.VMEM((KV_HEADS, KEY_ROWS, 2 * LANES), BF16),
        pltpu.VMEM((NBLK * KV_HEADS, PAIRS_PER_KV * ATT_BLOCK, KEY_COLS), BF16),
        pltpu.VMEM((tb + SUBLANES, QK_CONV_COLS), F32),
        pltpu.VMEM((M_HEADS, M_QK_DIM, M_V_DIM), F32),
        pltpu.VMEM((M_HEADS, SUBLANES, LANES), F32),
        pltpu.VMEM((SUBLANES, LANES), F32),
    ]
    return pl.pallas_call(
        functools.partial(_layer_kernel, nt=nt, n_tiles=batch * nt, first=first),
        out_shape=jax.ShapeDtypeStruct((batch, out_rows, d), h.dtype),
        grid=(batch * nt + 2,),
        in_specs=in_specs,
        out_specs=pl.BlockSpec((1, tb, d), lambda s: out_tile(s - 2)),
        scratch_shapes=scratch,
        compiler_params=pltpu.CompilerParams(
            dimension_semantics=("arbitrary",),
            vmem_limit_bytes=VMEM_LIMIT_BYTES),
        name="hybrid_layer",
    )(sink, h, h, head, rc, rs, ng, win, qg, kg, bd, cw, cb, bi, bf, og, wout)


def _rope_tables(lp):
    pos = jnp.arange(lp, dtype=F32) - PAD
    inv_freq = ROPE_THETA ** (-jnp.arange(0, ROT_DIM, 2, dtype=F32) / ROT_DIM)
    ang = pos[:, None] * inv_freq[None, :]
    cos, sin = jnp.cos(ang), jnp.sin(ang)
    ones = jnp.ones((lp, HEAD_DIM - ROT_DIM), F32)
    zeros = jnp.zeros((lp, HEAD_DIM - ROT_DIM), F32)
    rc = jnp.concatenate([cos, cos, ones], axis=1)
    rs = jnp.concatenate([-sin, sin, zeros], axis=1)
    return jnp.tile(rc, (1, 2)), jnp.tile(rs, (1, 2))


def _pad_lanes(v):
    return jnp.pad(v.astype(F32), (0, LANES - v.shape[0]))[None, :]


def kernel(x, meta, norm_g, w_in, attn_q_norm_g, attn_k_norm_g, attn_sink, mlstm_conv_w,
           mlstm_conv_b, mlstm_b_i, mlstm_b_f, mlstm_out_norm_g, w_out):
    batch, seq, d = x.shape
    depth = w_in.shape[0]
    assert d == D_MODEL and seq % ROW_TILE == 0
    lp = seq + LEAD
    head = jnp.concatenate([jnp.zeros((PAD, d), x.dtype), meta.astype(x.dtype)], axis=0)
    h = x
    rc, rs = _rope_tables(lp)
    head_id = jnp.arange(2 * LANES) // HEAD_DIM
    ones_bd = (head_id[:, None] == head_id[None, :]).astype(BF16)

    gate_lo = OFF_MO
    n_gate = 2 * M_HEADS
    lane_pad = ((0, 0), (0, 0), (0, LANES - M_HEADS))
    gi = jnp.pad(w_in[:, :, gate_lo:gate_lo + M_HEADS], lane_pad)
    gf = jnp.pad(w_in[:, :, gate_lo + M_HEADS:gate_lo + n_gate], lane_pad)
    win = jnp.concatenate([w_in[:, :, :gate_lo], w_in[:, :, gate_lo + n_gate:], gi, gf], axis=2).astype(BF16)
    wout = jnp.pad(w_out.astype(BF16), ((0, 0), (0, 0), (0, WOUT_PAD_COLS)))
    for l in range(depth):
        h = _layer_call(
            h, head, attn_sink[l].astype(F32) * LOG2_E, rc, rs,
            norm_g[l].astype(F32)[None, :], win,
            (jnp.tile(attn_q_norm_g[l].astype(F32), 2) * (HEAD_DIM ** -0.5 * LOG2_E))[None, :],
            jnp.tile(attn_k_norm_g[l].astype(F32), 2)[None, :], ones_bd,
            mlstm_conv_w[l].astype(F32), mlstm_conv_b[l].astype(F32)[None, :],
            _pad_lanes(mlstm_b_i[l]), _pad_lanes(mlstm_b_f[l]),
            mlstm_out_norm_g[l].astype(F32)[None, :], wout,
            layer=l, first=(l == 0), last=(l == depth - 1))
    return h
```

```python
import functools

import jax
import jax.numpy as jnp
from jax import lax
from jax.experimental import pallas as pl
from jax.experimental.pallas import tpu as pltpu

D_MODEL = 1024
N_META = 16
HEAD_DIM = 64
ATT_HEADS = 16
KV_HEADS = 2
ATT_BLOCK = 128
ROT_DIM = 16
ROPE_THETA = 500000.0
M_HEADS = 4
M_V_DIM = 256
M_QK_DIM = 128
CHUNK = 64
CONV_K = 4
EPS = 1e-6
NEG = -1e30
LOG2_E = 1.4426950408889634
NEG_LOG2_E = -LOG2_E

LANES = 128
SUBLANES = 8
MXU_COLS = 256
HEAD_PAIRS = ATT_HEADS // 2
PAIRS_PER_KV = HEAD_PAIRS // KV_HEADS
KEY_COLS = 5 * ATT_BLOCK
META_COL = 4 * ATT_BLOCK
SINK_LANE = 2 * N_META

ROW_TILE = 128
LEAD = ROW_TILE
PAD = LEAD - N_META
META_BLK = PAD // ATT_BLOCK
META_OFF = PAD % ATT_BLOCK
NBLK = ROW_TILE // ATT_BLOCK
BOT_ROW = (NBLK + 1) * ATT_BLOCK
META_ROW = 2 * BOT_ROW
KEY_ROWS = META_ROW + ATT_BLOCK

OFF_Q = 0
OFF_K = 1024
OFF_V = 1152
OFF_AG = 1280
OFF_MQ = 2304
OFF_MK = 2816
OFF_MV = 3328
OFF_MO = 4352
OFF_MG = 5376
OFF_GI = 6400
OFF_GF = 6528
IN_COLS = 6656
QK_CONV_COLS = 2 * M_HEADS * M_QK_DIM
MIX_COLS = 2 * D_MODEL

IN_JOB_COLS = MXU_COLS
OUT_JOB_COLS = MXU_COLS
TAIL_JOBS = 3
WOUT_PAD_COLS = LANES
VMEM_LIMIT_BYTES = 56 * 1024 * 1024

F32 = jnp.float32
BF16 = jnp.bfloat16


def _sigmoid(x):
    return 1.0 / (1.0 + jnp.exp2(x * NEG_LOG2_E))


def _silu(x):
    return x * _sigmoid(x)


def _log_sigmoid(x):
    return jnp.minimum(x, 0.0) - jnp.log1p(jnp.exp(-jnp.abs(x)))


def _dot(a, b):
    return jnp.dot(a, b, preferred_element_type=F32)


def _dot_nt(a, b):
    return lax.dot_general(a, b, (((1,), (1,)), ((), ())), preferred_element_type=F32)


def _dot_tn(a, b):
    return lax.dot_general(a, b, (((0,), (0,)), ((), ())), preferred_element_type=F32)


def _row_scan(x, row, combine, fill):
    shift = 1
    while shift < x.shape[0]:
        x = combine(x, jnp.where(row >= shift, pltpu.roll(x, shift, axis=0), fill))
        shift *= 2
    return x


def _interleave(a, b):
    out, ia, ib = [], 0, 0
    while ia < len(a) or ib < len(b):
        if ib >= len(b) or (ia < len(a) and ia * len(b) <= ib * len(a)):
            out.append(a[ia]); ia += 1
        else:
            out.append(b[ib]); ib += 1
    return out


def _layer_kernel(sink_ref, hn_ref, h_ref, head_ref, rc_ref, rs_ref, ng_ref, wa_ref, wb_ref, wg_ref,
                  qg_ref, kg_ref, bd_ref,
                  cw_ref, cb_ref, bi_ref, bf_ref, og_ref, wout_ref, o_ref,
                  za_ref, zb_ref, ya_ref, yb_ref, hb_ref, kk_ref, vv_ref, p_ref, u_ref,
                  c_ref, n_ref, m_ref, *, nt, n_tiles, first):
    s = pl.program_id(0)

    @pl.when(s == 0)
    def _():
        zb_ref[...] = jnp.zeros_like(zb_ref)
        yb_ref[...] = jnp.zeros_like(yb_ref)

    @pl.when((s == 0) | (lax.rem(s - 1, nt) == 0))
    def _():
        kk_ref[...] = jnp.zeros_like(kk_ref)
        r = lax.broadcasted_iota(jnp.int32, (KEY_ROWS, LANES), 0)
        l = lax.broadcasted_iota(jnp.int32, (KEY_ROWS, LANES), 1)
        m = r - META_ROW
        top = (r < BOT_ROW) | ((m >= 0) & (m < N_META)) | (m == SINK_LANE)
        bot = ((r >= BOT_ROW) & (r < META_ROW)) | ((m >= N_META) & (m < SINK_LANE)) | (m == SINK_LANE + 1)
        ones = jnp.where(l < HEAD_DIM, jnp.where(top, 1.0, 0.0), jnp.where(bot, 1.0, 0.0)).astype(BF16)
        for c in range(KV_HEADS):
            vv_ref[c, :, 0:LANES] = jnp.zeros((KEY_ROWS, LANES), BF16)
            vv_ref[c, :, LANES:2 * LANES] = ones
        u_ref[ROW_TILE:ROW_TILE + SUBLANES, :] = jnp.zeros((SUBLANES, QK_CONV_COLS), F32)
        c_ref[...] = jnp.zeros_like(c_ref)
        n_ref[...] = jnp.zeros_like(n_ref)
        m_ref[...] = jnp.zeros_like(m_ref)

    step = functools.partial(
        _layer_step, sink_ref, hn_ref, h_ref, head_ref, rc_ref, rs_ref, ng_ref, (wa_ref, wb_ref, wg_ref),
        qg_ref, kg_ref, bd_ref,
        cw_ref, cb_ref, bi_ref, bf_ref, og_ref, wout_ref, o_ref,
        hb_ref, kk_ref, vv_ref, p_ref, u_ref, c_ref, n_ref, m_ref, nt=nt, n_tiles=n_tiles, first=first)

    @pl.when(s % 2 == 0)
    def _():
        step(za_ref, zb_ref, ya_ref, yb_ref)

    @pl.when(s % 2 == 1)
    def _():
        step(zb_ref, za_ref, yb_ref, ya_ref)


def _layer_step(sink_ref, hn_ref, h_ref, head_ref, rc_ref, rs_ref, ng_ref, win_refs, qg_ref, kg_ref, bd_ref,
                cw_ref, cb_ref, bi_ref, bf_ref, og_ref, wout_ref, o_ref,
                hb_ref, kk_ref, vv_ref, p_ref, u_ref, c_ref, n_ref, m_ref,
                zw_ref, z_ref, yw_ref, yr_ref, *, nt, n_tiles, first):
    tb = ROW_TILE
    s = pl.program_id(0)
    i = jnp.where(s < 1, -1, lax.rem(s - 1, nt))
    i_out = jnp.where(s < 2, -1, lax.rem(s - 2, nt))
    nchunk = tb // CHUNK

    hx = hn_ref[0]
    if first:
        hx = jnp.where(lax.rem(jnp.minimum(s, n_tiles - 1), nt) == 0, head_ref[...], hx)
    hb_ref[...] = (hx * lax.rsqrt(jnp.mean(hx * hx, axis=-1, keepdims=True) + EPS)
                   * ng_ref[...]).astype(BF16)
    orow = i_out * tb + lax.broadcasted_iota(jnp.int32, (tb, 1), 0)

    def in_proj_job(c0):
        ref, base = [(r, b) for r, b in zip(win_refs, (0, OFF_MO, OFF_GI)) if b <= c0][-1]

        def run():
            zw_ref[:, c0:c0 + IN_JOB_COLS] = _dot(hb_ref[...], ref[:, c0 - base:c0 - base + IN_JOB_COLS])
        return run

    def out_proj_job(c0):
        def run():
            y = (_dot(yr_ref[:, 0:D_MODEL], wout_ref[0:D_MODEL, c0:c0 + OUT_JOB_COLS])
                 + _dot(yr_ref[:, D_MODEL:MIX_COLS], wout_ref[D_MODEL:MIX_COLS, c0:c0 + OUT_JOB_COLS]))
            res = h_ref[0, :, c0:c0 + OUT_JOB_COLS]
            if first:
                res = jnp.where(i_out == 0, head_ref[:, c0:c0 + OUT_JOB_COLS], res)
            o_ref[0, :, c0:c0 + OUT_JOB_COLS] = jnp.where(orow >= PAD, res + y, 0.0)
        return run

    jobs = _interleave([in_proj_job(c0) for c0 in range(0, IN_COLS, IN_JOB_COLS)],
                       [out_proj_job(c0) for c0 in range(0, D_MODEL, OUT_JOB_COLS)])

    lane = lax.broadcasted_iota(jnp.int32, (ATT_BLOCK, LANES), 1)
    lo_half = lane < HEAD_DIM
    rot_lo = (lane % HEAD_DIM) < (ROT_DIM // 2)

    def head_rms_inv(x):
        w = x.shape[1]
        ss = _dot((x * x).astype(BF16), bd_ref[0:w, 0:w])
        return lax.rsqrt(ss * (1.0 / HEAD_DIM) + EPS)

    def rope(xn, rc, rs):
        partner = jnp.where(rot_lo, pltpu.roll(xn, LANES - ROT_DIM // 2, axis=1),
                            pltpu.roll(xn, ROT_DIM // 2, axis=1))
        return xn * rc + partner * rs

    def keys_of(ref, c, j):
        top = ref[c, j * ATT_BLOCK:(j + 2) * ATT_BLOCK, :]
        bot = ref[c, BOT_ROW + j * ATT_BLOCK:BOT_ROW + (j + 2) * ATT_BLOCK, :]
        return jnp.concatenate([top, bot, ref[c, META_ROW:META_ROW + ATT_BLOCK, :]], axis=0)

    attn_units, mlstm_units = [], []

    for j in range(NBLK):
        r0 = j * ATT_BLOCK
        blk = i * NBLK + j
        ctx = {}

        def kv_update(j=j, r0=r0, blk=blk, ctx=ctx):
            rc = rc_ref[r0:r0 + ATT_BLOCK, :]
            rs = rs_ref[r0:r0 + ATT_BLOCK, :]
            kx = z_ref[r0:r0 + ATT_BLOCK, OFF_K:OFF_K + LANES]
            kp = rope(kx * head_rms_inv(kx) * kg_ref[...], rc, rs)
            vp = z_ref[r0:r0 + ATT_BLOCK, OFF_V:OFF_V + LANES]
            kp_sw = pltpu.roll(kp, HEAD_DIM, axis=1)
            vp_sw = pltpu.roll(vp, HEAD_DIM, axis=1)
            slot = (j + 1) * ATT_BLOCK
            for c in range(KV_HEADS):
                for ref, own, swapped in ((kk_ref, kp, kp_sw), (vv_ref, vp, vp_sw)):
                    top = jnp.where(lo_half, own if c == 0 else swapped, 0.0).astype(BF16)
                    bot = jnp.where(lo_half, 0.0, swapped if c == 0 else own).astype(BF16)
                    ref[c, slot:slot + ATT_BLOCK, 0:LANES] = top
                    ref[c, BOT_ROW + slot:BOT_ROW + slot + ATT_BLOCK, 0:LANES] = bot
                    if j == META_BLK % NBLK:
                        for m0, part in ((META_ROW, top), (META_ROW + N_META, bot)):
                            keep = ref[c, m0:m0 + N_META, 0:LANES].astype(F32)
                            new = part[META_OFF:ATT_BLOCK, :].astype(F32)
                            ref[c, m0:m0 + N_META, 0:LANES] = jnp.where(blk == META_BLK, new, keep).astype(BF16)

            qrow = blk * ATT_BLOCK + lax.broadcasted_iota(jnp.int32, (ATT_BLOCK, 2 * ATT_BLOCK), 0)
            krow = (blk - 1) * ATT_BLOCK + lax.broadcasted_iota(jnp.int32, (ATT_BLOCK, 2 * ATT_BLOCK), 1)
            dist = qrow - krow
            ctx["band_ok"] = (krow >= LEAD) & (dist >= 0) & (dist < ATT_BLOCK)
            qrow_m = blk * ATT_BLOCK + lax.broadcasted_iota(jnp.int32, (ATT_BLOCK, LANES), 0)
            meta_ok = (PAD + (lane % N_META)) <= qrow_m
            ctx["meta_ok"] = ((lane < N_META) & meta_ok,
                              (lane >= N_META) & (lane < 2 * N_META) & meta_ok)
            ctx["meta_any"] = (lane < 2 * N_META) & meta_ok
            ctx["sink_lanes"] = (lane >> 1) == (SINK_LANE >> 1)
            ctx["first_head_lanes"] = (lane < N_META) | (lane == SINK_LANE)

        def q_scores(c, j=j, r0=r0, ctx=ctx):
            rc = rc_ref[r0:r0 + ATT_BLOCK, :]
            rs = rs_ref[r0:r0 + ATT_BLOCK, :]
            gain2 = jnp.concatenate([qg_ref[...], qg_ref[...]], axis=1)
            q4 = []
            for half in range(PAIRS_PER_KV // 2):
                c0 = OFF_Q + (c * PAIRS_PER_KV + 2 * half) * LANES
                xq = z_ref[r0:r0 + ATT_BLOCK, c0:c0 + 2 * LANES]
                xn = xq * head_rms_inv(xq) * gain2
                q4.append(rope(xn[:, 0:LANES], rc, rs).astype(BF16))
                q4.append(rope(xn[:, LANES:2 * LANES], rc, rs).astype(BF16))
            ctx["s4", c] = _dot_nt(jnp.concatenate(q4, axis=0), keys_of(kk_ref, c, j))

        def softmax(c, pi, j=j, ctx=ctx):
            p = c * PAIRS_PER_KV + pi
            pbuf = p_ref.at[j * KV_HEADS + c]
            sc = ctx["s4", c][pi * ATT_BLOCK:(pi + 1) * ATT_BLOCK, :]
            sm = sc[:, META_COL:META_COL + LANES]
            rows = slice(pi * ATT_BLOCK, (pi + 1) * ATT_BLOCK)
            mxs = []
            for hh in range(2):
                sink = sink_ref[2 * p + hh]
                sb = jnp.where(ctx["band_ok"], sc[:, hh * 2 * ATT_BLOCK:(hh + 1) * 2 * ATT_BLOCK], NEG)
                smh = jnp.where(ctx["meta_ok"][hh], sm, NEG)
                both = jnp.maximum(jnp.maximum(sb[:, 0:LANES], sb[:, LANES:2 * LANES]), smh)
                mx = jnp.maximum(jnp.max(both, axis=-1, keepdims=True), sink)
                mxs.append(mx)
                pbuf[rows, hh * 2 * ATT_BLOCK:(hh + 1) * 2 * ATT_BLOCK] = jnp.exp2(sb - mx).astype(BF16)
            sink_pair = jnp.where(lane == SINK_LANE, sink_ref[2 * p], sink_ref[2 * p + 1])
            sm2 = jnp.where(ctx["meta_any"], sm, jnp.where(ctx["sink_lanes"], sink_pair, NEG))
            mx2 = jnp.where(ctx["first_head_lanes"], mxs[0], mxs[1])
            pbuf[rows, META_COL:META_COL + LANES] = jnp.exp2(sm2 - mx2).astype(BF16)

        def pv(c, j=j, r0=r0):
            o4 = _dot(p_ref[j * KV_HEADS + c], keys_of(vv_ref, c, j))
            for pi in range(PAIRS_PER_KV):
                p = c * PAIRS_PER_KV + pi
                rows = slice(pi * ATT_BLOCK, (pi + 1) * ATT_BLOCK)
                gate = _silu(z_ref[r0:r0 + ATT_BLOCK, OFF_AG + p * LANES:OFF_AG + (p + 1) * LANES])
                yw_ref[r0:r0 + ATT_BLOCK, p * LANES:(p + 1) * LANES] = (
                    o4[rows, 0:LANES] * (1.0 / o4[rows, LANES:2 * LANES]) * gate).astype(BF16)

        attn_units.append((kv_update, 1))
        for c in range(KV_HEADS):
            attn_units.append((functools.partial(q_scores, c), 0))
            attn_units.extend((functools.partial(softmax, c, pi), 1) for pi in range(PAIRS_PER_KV))
            attn_units.append((functools.partial(pv, c), 0))

    def kv_carry():
        for ref in (kk_ref, vv_ref):
            for c in range(KV_HEADS):
                for base in (0, BOT_ROW):
                    ref[c, base:base + ATT_BLOCK, 0:LANES] = (
                        ref[c, base + NBLK * ATT_BLOCK:base + (NBLK + 1) * ATT_BLOCK, 0:LANES])

    crow = lax.broadcasted_iota(jnp.int32, (CHUNK, LANES), 0)
    tril = (lax.broadcasted_iota(jnp.int32, (CHUNK, CHUNK), 0)
            >= lax.broadcasted_iota(jnp.int32, (CHUNK, CHUNK), 1))
    for cj in range(nchunk):
        q0 = cj * CHUNK
        g = {}

        def chunk_prep(cj=cj, q0=q0, g=g):
            if cj == 0:
                u_ref[0:SUBLANES, :] = u_ref[tb:tb + SUBLANES, :]
                u_ref[SUBLANES:SUBLANES + tb, :] = z_ref[:, OFF_MQ:OFF_MQ + QK_CONV_COLS]
            real =(i * tb + q0 + crow) >= PAD
            real_col = (i * tb + q0 + lax.broadcasted_iota(jnp.int32, (CHUNK, 1), 0)) >= PAD
            uu = u_ref[q0:q0 + SUBLANES + CHUNK, :]
            acc = cw_ref[0:1, :] * uu
            for t in range(1, CONV_K):
                acc = cw_ref[t:t + 1, :] * uu + pltpu.roll(acc, 1, axis=0)
            conv = acc[SUBLANES:, :] + cb_ref[...]
            g["qk"] = jnp.where(real_col, _silu(conv), 0.0)

            li = jnp.where(real, z_ref[q0:q0 + CHUNK, OFF_GI:OFF_GI + LANES] + bi_ref[...], NEG)
            lf = jnp.where(real, _log_sigmoid(z_ref[q0:q0 + CHUNK, OFF_GF:OFF_GF + LANES] + bf_ref[...]), 0.0)
            b = _row_scan(lf, crow, jnp.add, 0.0)
            r = li - b
            rmax = jnp.max(r, axis=0, keepdims=True)
            g["wg"] = jnp.exp(r - rmax)
            dmax = b + _row_scan(r, crow, jnp.maximum, NEG)
            m_in = m_ref[0:1, :]
            a_t = b + m_in
            m_t = jnp.maximum(a_t, dmax)
            g["inter"] = jnp.exp(a_t - m_t)
            g["cvec"] = b - m_t
            g["em_t"] = jnp.exp(-m_t)
            g["r_t"] = r.T
            b_last = b[CHUNK - 1:CHUNK, :]
            mg = b_last + rmax
            m_new = jnp.maximum(b_last + m_in, mg)
            g["a_st"] = jnp.exp(b_last + m_in - m_new)
            g["w_st"] = jnp.exp(mg - m_new)
            m_ref[...] = jnp.broadcast_to(m_new, (SUBLANES, LANES))

        def heads_state(q0=q0, g=g):
            for hd in range(M_HEADS):
                qk = g["qk"]
                q = qk[:, hd * M_QK_DIM:(hd + 1) * M_QK_DIM]
                k = qk[:, (M_HEADS + hd) * M_QK_DIM:(M_HEADS + hd + 1) * M_QK_DIM] * (M_QK_DIM ** -0.5)
                vb = z_ref[q0:q0 + CHUNK, OFF_MV + hd * M_V_DIM:OFF_MV + (hd + 1) * M_V_DIM].astype(BF16)
                qb = q.astype(BF16)
                c_in = c_ref[hd]
                n_in = n_ref[hd, 0:1, :]
                g["s", hd] = _dot_nt(qb, k.astype(BF16))
                g["qc", hd] = _dot(qb, c_in.astype(BF16))
                g["qn", hd] = jnp.sum(q * n_in, axis=-1, keepdims=True)
                g["vb", hd] = vb

                kw = g["wg"][:, hd:hd + 1] * k
                a_s = g["a_st"][:, hd:hd + 1]
                w_s = g["w_st"][:, hd:hd + 1]
                c_ref[hd] = a_s * c_in + w_s * _dot_tn(kw.astype(BF16), vb)
                n_new = a_s * n_in + w_s * jnp.sum(kw, axis=0, keepdims=True)
                n_ref[hd] = jnp.broadcast_to(n_new, (SUBLANES, LANES))

        def head_out(hd, q0=q0, g=g):
            arg = g["cvec"][:, hd:hd + 1] + g["r_t"][hd:hd + 1, :]
            sg = g["s", hd] * jnp.exp(jnp.where(tril, arg, NEG))
            inter_c = g["inter"][:, hd:hd + 1]
            num = inter_c * g["qc", hd] + _dot(sg.astype(BF16), g["vb", hd])
            den = inter_c * g["qn", hd] + jnp.sum(sg, axis=-1, keepdims=True)
            hout = num / jnp.maximum(jnp.abs(den), g["em_t"][:, hd:hd + 1])

            cols = slice(hd * M_V_DIM, (hd + 1) * M_V_DIM)
            hm = hout * lax.rsqrt(jnp.mean(hout * hout, axis=-1, keepdims=True) + EPS) * og_ref[:, cols]
            ocols = slice(OFF_MO + hd * M_V_DIM, OFF_MO + (hd + 1) * M_V_DIM)
            gcols = slice(OFF_MG + hd * M_V_DIM, OFF_MG + (hd + 1) * M_V_DIM)
            ym = hm * _sigmoid(z_ref[q0:q0 + CHUNK, ocols]) * _silu(z_ref[q0:q0 + CHUNK, gcols])
            yw_ref[q0:q0 + CHUNK, D_MODEL + hd * M_V_DIM:D_MODEL + (hd + 1) * M_V_DIM] = ym.astype(BF16)

        mlstm_units.append((chunk_prep, 1))
        mlstm_units.append((heads_state, 0))
        mlstm_units.extend((functools.partial(head_out, hd), 1) for hd in range(M_HEADS))

    kv_carry()
    units = _interleave(attn_units, mlstm_units)
    n_receivers = sum(takes for _, takes in units)
    n_spread = len(jobs) - TAIL_JOBS
    seen = 0
    for unit, takes in units:
        seen += takes
        while takes and len(jobs) > TAIL_JOBS and (n_spread + TAIL_JOBS - len(jobs)) * n_receivers < seen * n_spread:
            jobs.pop(0)()
        unit()
    for job in jobs:
        job()


def _layer_call(h, head, sink, rc, rs, ng, win, qg, kg, bd, cw, cb, bi, bf, og, wout, *, layer, first, last):
    batch, rows, d = h.shape
    tb = ROW_TILE
    nt = (rows + LEAD if first else rows) // tb
    last_tile = batch * nt - 1

    def tile_of(t):
        t = jnp.clip(t, 0, last_tile)
        return t // nt, t % nt, 0

    def token_tile_of(t):
        b, i, _ = tile_of(t)
        return b, jnp.maximum(i - LEAD // tb, 0), 0

    const = lambda s: (0, 0)
    in_tile = token_tile_of if first else tile_of
    mix_rows = lambda s: tile_of(s - 1)[1:]
    out_tile = token_tile_of if last else tile_of
    out_rows = rows + (LEAD if first else 0) - (LEAD if last else 0)
    once = pl.Buffered(1)
    full = lambda a: pl.BlockSpec(a.shape, const)
    in_specs = [
        pl.BlockSpec(memory_space=pltpu.SMEM),
        pl.BlockSpec((1, tb, d), lambda s: in_tile(s)),
        pl.BlockSpec((1, tb, d), lambda s: in_tile(s - 2)),
        full(head),
        pl.BlockSpec((tb, LANES), mix_rows),
        pl.BlockSpec((tb, LANES), mix_rows),
        full(ng),
        *[pl.BlockSpec((None,) + w.shape[1:], lambda s: (layer, 0, 0), pipeline_mode=once) for w in win],
        full(qg), full(kg), full(bd), full(cw), full(cb), full(bi), full(bf), full(og),
        pl.BlockSpec((None,) + wout.shape[1:], lambda s: (layer, 0, 0), pipeline_mode=once),
    ]
    scratch = [
        pltpu.VMEM((tb, IN_COLS), F32),
        pltpu.VMEM((tb, IN_COLS), F32),
        pltpu.VMEM((tb, MIX_COLS), BF16),
        pltpu.VMEM((tb, MIX_COLS), BF16),
        pltpu.VMEM((tb, D_MODEL), BF16),
        pltpu.VMEM((KV_HEADS, KEY_ROWS, LANES), BF16),
        pltpu.VMEM((KV_HEADS, KEY_ROWS, 2 * LANES), BF16),
        pltpu.VMEM((NBLK * KV_HEADS, PAIRS_PER_KV * ATT_BLOCK, KEY_COLS), BF16),
        pltpu.VMEM((tb + SUBLANES, QK_CONV_COLS), F32),
        pltpu.VMEM((M_HEADS, M_QK_DIM, M_V_DIM), F32),
        pltpu.VMEM((M_HEADS, SUBLANES, LANES), F32),
        pltpu.VMEM((SUBLANES, LANES), F32),
    ]
    return pl.pallas_call(
        functools.partial(_layer_kernel, nt=nt, n_tiles=batch * nt, first=first),
        out_shape=jax.ShapeDtypeStruct((batch, out_rows, d), h.dtype),
        grid=(batch * nt + 2,),
        in_specs=in_specs,
        out_specs=pl.BlockSpec((1, tb, d), lambda s: out_tile(s - 2)),
        scratch_shapes=scratch,
        compiler_params=pltpu.CompilerParams(
            dimension_semantics=("arbitrary",),
            vmem_limit_bytes=VMEM_LIMIT_BYTES),
        name="hybrid_layer",
    )(sink, h, h, head, rc, rs, ng, *win, qg, kg, bd, cw, cb, bi, bf, og, wout)


def _rope_tables(lp):
    pos = jnp.arange(lp, dtype=F32) - PAD
    inv_freq = ROPE_THETA ** (-jnp.arange(0, ROT_DIM, 2, dtype=F32) / ROT_DIM)
    ang = pos[:, None] * inv_freq[None, :]
    cos, sin = jnp.cos(ang), jnp.sin(ang)
    ones = jnp.ones((lp, HEAD_DIM - ROT_DIM), F32)
    zeros = jnp.zeros((lp, HEAD_DIM - ROT_DIM), F32)
    rc = jnp.concatenate([cos, cos, ones], axis=1)
    rs = jnp.concatenate([-sin, sin, zeros], axis=1)
    return jnp.tile(rc, (1, 2)), jnp.tile(rs, (1, 2))


def _pad_lanes(v):
    return jnp.pad(v.astype(F32), (0, LANES - v.shape[0]))[None, :]


def kernel(x, meta, norm_g, w_in, attn_q_norm_g, attn_k_norm_g, attn_sink, mlstm_conv_w,
           mlstm_conv_b, mlstm_b_i, mlstm_b_f, mlstm_out_norm_g, w_out):
    batch, seq, d = x.shape
    depth = w_in.shape[0]
    assert d == D_MODEL and seq % ROW_TILE == 0
    lp = seq + LEAD
    head = jnp.concatenate([jnp.zeros((PAD, d), x.dtype), meta.astype(x.dtype)], axis=0)
    h = x
    rc, rs = _rope_tables(lp)
    head_id = jnp.arange(2 * LANES) // HEAD_DIM
    ones_bd = (head_id[:, None] == head_id[None, :]).astype(BF16)

    gate_lo = OFF_MO
    n_gate = 2 * M_HEADS
    lane_pad = ((0, 0), (0, 0), (0, LANES - M_HEADS))
    gi = jnp.pad(w_in[:, :, gate_lo:gate_lo + M_HEADS], lane_pad)
    gf = jnp.pad(w_in[:, :, gate_lo + M_HEADS:gate_lo + n_gate], lane_pad)
    stride_pad = ((0, 0), (0, 0), (0, WOUT_PAD_COLS))
    win = (w_in[:, :, :gate_lo].astype(BF16),
           jnp.pad(w_in[:, :, gate_lo + n_gate:].astype(BF16), stride_pad),
           jnp.pad(jnp.concatenate([gi, gf], axis=2).astype(BF16), stride_pad))
    wout = jnp.pad(w_out.astype(BF16), ((0, 0), (0, 0), (0, WOUT_PAD_COLS)))
    for l in range(depth):
        h = _layer_call(
            h, head, attn_sink[l].astype(F32) * LOG2_E, rc, rs,
            norm_g[l].astype(F32)[None, :], win,
            (jnp.tile(attn_q_norm_g[l].astype(F32), 2) * (HEAD_DIM ** -0.5 * LOG2_E))[None, :],
            jnp.tile(attn_k_norm_g[l].astype(F32), 2)[None, :], ones_bd,
            mlstm_conv_w[l].astype(F32), mlstm_conv_b[l].astype(F32)[None, :],
            _pad_lanes(mlstm_b_i[l]), _pad_lanes(mlstm_b_f[l]),
            mlstm_out_norm_g[l].astype(F32)[None, :], wout,
            layer=l, first=(l == 0), last=(l == depth - 1))
    return h
```

```python
import functools

import jax
import jax.numpy as jnp
from jax import lax
from jax.experimental import pallas as pl
from jax.experimental.pallas import tpu as pltpu

D_MODEL = 1024
N_META = 16
HEAD_DIM = 64
ATT_HEADS = 16
KV_HEADS = 2
ATT_BLOCK = 128
ROT_DIM = 16
ROPE_THETA = 500000.0
M_HEADS = 4
M_V_DIM = 256
M_QK_DIM = 128
CHUNK = 64
CONV_K = 4
EPS = 1e-6
NEG = -1e30
LOG2_E = 1.4426950408889634
NEG_LOG2_E = -LOG2_E

LANES = 128
SUBLANES = 8
MXU_COLS = 256
HEAD_PAIRS = ATT_HEADS // 2
PAIRS_PER_KV = HEAD_PAIRS // KV_HEADS
KEY_COLS = 5 * ATT_BLOCK
META_COL = 4 * ATT_BLOCK
SINK_LANE = 2 * N_META

ROW_TILE = 128
LEAD = ROW_TILE
PAD = LEAD - N_META
META_BLK = PAD // ATT_BLOCK
META_OFF = PAD % ATT_BLOCK
NBLK = ROW_TILE // ATT_BLOCK
BOT_ROW = (NBLK + 1) * ATT_BLOCK
META_ROW = 2 * BOT_ROW
KEY_ROWS = META_ROW + ATT_BLOCK

OFF_Q = 0
OFF_K = 1024
OFF_V = 1152
OFF_AG = 1280
OFF_MQ = 2304
OFF_MK = 2816
OFF_MV = 3328
OFF_MO = 4352
OFF_MG = 5376
OFF_GI = 6400
OFF_GF = 6528
IN_COLS = 6656
QK_CONV_COLS = 2 * M_HEADS * M_QK_DIM
MIX_COLS = 2 * D_MODEL

IN_JOB_COLS = MXU_COLS
OUT_JOB_COLS = MXU_COLS
TAIL_JOBS = 1
UNIT_ORDER_A = (0, 1, 3, 5, 7, 9, 12, 10, 14, 16, 18, 20, 22)
UNIT_ORDER_M = (2, 4, 6, 8, 11, 13, 15, 17, 19, 21, 23, 24)
WOUT_PAD_COLS = LANES
VMEM_LIMIT_BYTES = 56 * 1024 * 1024

F32 = jnp.float32
BF16 = jnp.bfloat16


def _sigmoid(x):
    return 1.0 / (1.0 + jnp.exp2(x * NEG_LOG2_E))


def _silu(x):
    return x * _sigmoid(x)


def _log_sigmoid(x):
    return jnp.minimum(x, 0.0) - jnp.log1p(jnp.exp(-jnp.abs(x)))


def _dot(a, b):
    return jnp.dot(a, b, preferred_element_type=F32)


def _dot_nt(a, b):
    return lax.dot_general(a, b, (((1,), (1,)), ((), ())), preferred_element_type=F32)


def _dot_tn(a, b):
    return lax.dot_general(a, b, (((0,), (0,)), ((), ())), preferred_element_type=F32)


def _row_scan(x, row, combine, fill):
    shift = 1
    while shift < x.shape[0]:
        x = combine(x, jnp.where(row >= shift, pltpu.roll(x, shift, axis=0), fill))
        shift *= 2
    return x


def _interleave(a, b):
    out, ia, ib = [], 0, 0
    while ia < len(a) or ib < len(b):
        if ib >= len(b) or (ia < len(a) and ia * len(b) <= ib * len(a)):
            out.append(a[ia]); ia += 1
        else:
            out.append(b[ib]); ib += 1
    return out


def _layer_kernel(sink_ref, hn_ref, h_ref, head_ref, rc_ref, rs_ref, ng_ref, wa_ref, wb_ref, wg_ref,
                  qg_ref, kg_ref, bd_ref,
                  cw_ref, cb_ref, bi_ref, bf_ref, og_ref, wout_ref, o_ref,
                  za_ref, zb_ref, ya_ref, yb_ref, hb_ref, kk_ref, vv_ref, p_ref, u_ref,
                  c_ref, n_ref, m_ref, *, nt, n_tiles, first):
    s = pl.program_id(0)

    @pl.when(s == 0)
    def _():
        zb_ref[...] = jnp.zeros_like(zb_ref)
        yb_ref[...] = jnp.zeros_like(yb_ref)

    @pl.when((s == 0) | (lax.rem(s - 1, nt) == 0))
    def _():
        kk_ref[...] = jnp.zeros_like(kk_ref)
        r = lax.broadcasted_iota(jnp.int32, (KEY_ROWS, LANES), 0)
        l = lax.broadcasted_iota(jnp.int32, (KEY_ROWS, LANES), 1)
        m = r - META_ROW
        top = (r < BOT_ROW) | ((m >= 0) & (m < N_META)) | (m == SINK_LANE)
        bot = ((r >= BOT_ROW) & (r < META_ROW)) | ((m >= N_META) & (m < SINK_LANE)) | (m == SINK_LANE + 1)
        ones = jnp.where(l < HEAD_DIM, jnp.where(top, 1.0, 0.0), jnp.where(bot, 1.0, 0.0)).astype(BF16)
        for c in range(KV_HEADS):
            vv_ref[c, :, 0:LANES] = jnp.zeros((KEY_ROWS, LANES), BF16)
            vv_ref[c, :, LANES:2 * LANES] = ones
        u_ref[ROW_TILE:ROW_TILE + SUBLANES, :] = jnp.zeros((SUBLANES, QK_CONV_COLS), F32)
        c_ref[...] = jnp.zeros_like(c_ref)
        n_ref[...] = jnp.zeros_like(n_ref)
        m_ref[...] = jnp.zeros_like(m_ref)

    step = functools.partial(
        _layer_step, sink_ref, hn_ref, h_ref, head_ref, rc_ref, rs_ref, ng_ref, (wa_ref, wb_ref, wg_ref),
        qg_ref, kg_ref, bd_ref,
        cw_ref, cb_ref, bi_ref, bf_ref, og_ref, wout_ref, o_ref,
        hb_ref, kk_ref, vv_ref, p_ref, u_ref, c_ref, n_ref, m_ref, nt=nt, n_tiles=n_tiles, first=first)

    @pl.when(s % 2 == 0)
    def _():
        step(za_ref, zb_ref, ya_ref, yb_ref)

    @pl.when(s % 2 == 1)
    def _():
        step(zb_ref, za_ref, yb_ref, ya_ref)


def _layer_step(sink_ref, hn_ref, h_ref, head_ref, rc_ref, rs_ref, ng_ref, win_refs, qg_ref, kg_ref, bd_ref,
                cw_ref, cb_ref, bi_ref, bf_ref, og_ref, wout_ref, o_ref,
                hb_ref, kk_ref, vv_ref, p_ref, u_ref, c_ref, n_ref, m_ref,
                zw_ref, z_ref, yw_ref, yr_ref, *, nt, n_tiles, first):
    tb = ROW_TILE
    s = pl.program_id(0)
    i = jnp.where(s < 1, -1, lax.rem(s - 1, nt))
    i_out = jnp.where(s < 2, -1, lax.rem(s - 2, nt))
    nchunk = tb // CHUNK

    hx = hn_ref[0]
    if first:
        hx = jnp.where(lax.rem(jnp.minimum(s, n_tiles - 1), nt) == 0, head_ref[...], hx)
    hb_ref[...] = (hx * lax.rsqrt(jnp.mean(hx * hx, axis=-1, keepdims=True) + EPS)
                   * ng_ref[...]).astype(BF16)
    orow = i_out * tb + lax.broadcasted_iota(jnp.int32, (tb, 1), 0)

    def in_proj_job(c0):
        ref, base = [(r, b) for r, b in zip(win_refs, (0, OFF_MO, OFF_GI)) if b <= c0][-1]

        def run():
            zw_ref[:, c0:c0 + IN_JOB_COLS] = _dot(hb_ref[...], ref[:, c0 - base:c0 - base + IN_JOB_COLS])
        return run

    def out_proj_job(c0):
        def run():
            y = (_dot(yr_ref[:, 0:D_MODEL], wout_ref[0:D_MODEL, c0:c0 + OUT_JOB_COLS])
                 + _dot(yr_ref[:, D_MODEL:MIX_COLS], wout_ref[D_MODEL:MIX_COLS, c0:c0 + OUT_JOB_COLS]))
            res = h_ref[0, :, c0:c0 + OUT_JOB_COLS]
            if first:
                res = jnp.where(i_out == 0, head_ref[:, c0:c0 + OUT_JOB_COLS], res)
            o_ref[0, :, c0:c0 + OUT_JOB_COLS] = jnp.where(orow >= PAD, res + y, 0.0)
        return run

    jobs = _interleave([in_proj_job(c0) for c0 in range(0, IN_COLS, IN_JOB_COLS)],
                       [out_proj_job(c0) for c0 in range(0, D_MODEL, OUT_JOB_COLS)])

    lane = lax.broadcasted_iota(jnp.int32, (ATT_BLOCK, LANES), 1)
    lo_half = lane < HEAD_DIM
    rot_lo = (lane % HEAD_DIM) < (ROT_DIM // 2)

    def head_rms_inv(x):
        w = x.shape[1]
        ss = _dot((x * x).astype(BF16), bd_ref[0:w, 0:w])
        return lax.rsqrt(ss * (1.0 / HEAD_DIM) + EPS)

    def rope(xn, rc, rs):
        partner = jnp.where(rot_lo, pltpu.roll(xn, LANES - ROT_DIM // 2, axis=1),
                            pltpu.roll(xn, ROT_DIM // 2, axis=1))
        return xn * rc + partner * rs

    def keys_of(ref, c, j):
        top = ref[c, j * ATT_BLOCK:(j + 2) * ATT_BLOCK, :]
        bot = ref[c, BOT_ROW + j * ATT_BLOCK:BOT_ROW + (j + 2) * ATT_BLOCK, :]
        return jnp.concatenate([top, bot, ref[c, META_ROW:META_ROW + ATT_BLOCK, :]], axis=0)

    attn_units, mlstm_units = [], []

    for j in range(NBLK):
        r0 = j * ATT_BLOCK
        blk = i * NBLK + j
        ctx = {}

        def kv_update(j=j, r0=r0, blk=blk, ctx=ctx):
            rc = rc_ref[r0:r0 + ATT_BLOCK, :]
            rs = rs_ref[r0:r0 + ATT_BLOCK, :]
            kx = z_ref[r0:r0 + ATT_BLOCK, OFF_K:OFF_K + LANES]
            kp = rope(kx * head_rms_inv(kx) * kg_ref[...], rc, rs)
            vp = z_ref[r0:r0 + ATT_BLOCK, OFF_V:OFF_V + LANES]
            kp_sw = pltpu.roll(kp, HEAD_DIM, axis=1)
            vp_sw = pltpu.roll(vp, HEAD_DIM, axis=1)
            slot = (j + 1) * ATT_BLOCK
            for c in range(KV_HEADS):
                for ref, own, swapped in ((kk_ref, kp, kp_sw), (vv_ref, vp, vp_sw)):
                    top = jnp.where(lo_half, own if c == 0 else swapped, 0.0).astype(BF16)
                    bot = jnp.where(lo_half, 0.0, swapped if c == 0 else own).astype(BF16)
                    ref[c, slot:slot + ATT_BLOCK, 0:LANES] = top
                    ref[c, BOT_ROW + slot:BOT_ROW + slot + ATT_BLOCK, 0:LANES] = bot
                    if j == META_BLK % NBLK:
                        for m0, part in ((META_ROW, top), (META_ROW + N_META, bot)):
                            keep = ref[c, m0:m0 + N_META, 0:LANES].astype(F32)
                            new = part[META_OFF:ATT_BLOCK, :].astype(F32)
                            ref[c, m0:m0 + N_META, 0:LANES] = jnp.where(blk == META_BLK, new, keep).astype(BF16)

            qrow = blk * ATT_BLOCK + lax.broadcasted_iota(jnp.int32, (ATT_BLOCK, 2 * ATT_BLOCK), 0)
            krow = (blk - 1) * ATT_BLOCK + lax.broadcasted_iota(jnp.int32, (ATT_BLOCK, 2 * ATT_BLOCK), 1)
            dist = qrow - krow
            ctx["band_ok"] = (krow >= LEAD) & (dist >= 0) & (dist < ATT_BLOCK)
            qrow_m = blk * ATT_BLOCK + lax.broadcasted_iota(jnp.int32, (ATT_BLOCK, LANES), 0)
            meta_ok = (PAD + (lane % N_META)) <= qrow_m
            ctx["meta_ok"] = ((lane < N_META) & meta_ok,
                              (lane >= N_META) & (lane < 2 * N_META) & meta_ok)
            ctx["meta_any"] = (lane < 2 * N_META) & meta_ok
            ctx["sink_lanes"] = (lane >> 1) == (SINK_LANE >> 1)
            ctx["first_head_lanes"] = (lane < N_META) | (lane == SINK_LANE)

        def q_scores(c, j=j, r0=r0, ctx=ctx):
            rc = rc_ref[r0:r0 + ATT_BLOCK, :]
            rs = rs_ref[r0:r0 + ATT_BLOCK, :]
            gain2 = jnp.concatenate([qg_ref[...], qg_ref[...]], axis=1)
            q4 = []
            for half in range(PAIRS_PER_KV // 2):
                c0 = OFF_Q + (c * PAIRS_PER_KV + 2 * half) * LANES
                xq = z_ref[r0:r0 + ATT_BLOCK, c0:c0 + 2 * LANES]
                xn = xq * head_rms_inv(xq) * gain2
                q4.append(rope(xn[:, 0:LANES], rc, rs).astype(BF16))
                q4.append(rope(xn[:, LANES:2 * LANES], rc, rs).astype(BF16))
            ctx["s4", c] = _dot_nt(jnp.concatenate(q4, axis=0), keys_of(kk_ref, c, j))

        def softmax(c, pi, j=j, ctx=ctx):
            p = c * PAIRS_PER_KV + pi
            pbuf = p_ref.at[j * KV_HEADS + c]
            sc = ctx["s4", c][pi * ATT_BLOCK:(pi + 1) * ATT_BLOCK, :]
            sm = sc[:, META_COL:META_COL + LANES]
            rows = slice(pi * ATT_BLOCK, (pi + 1) * ATT_BLOCK)
            mxs = []
            for hh in range(2):
                sink = sink_ref[2 * p + hh]
                sb = jnp.where(ctx["band_ok"], sc[:, hh * 2 * ATT_BLOCK:(hh + 1) * 2 * ATT_BLOCK], NEG)
                smh = jnp.where(ctx["meta_ok"][hh], sm, NEG)
                both = jnp.maximum(jnp.maximum(sb[:, 0:LANES], sb[:, LANES:2 * LANES]), smh)
                mx = jnp.maximum(jnp.max(both, axis=-1, keepdims=True), sink)
                mxs.append(mx)
                pbuf[rows, hh * 2 * ATT_BLOCK:(hh + 1) * 2 * ATT_BLOCK] = jnp.exp2(sb - mx).astype(BF16)
            sink_pair = jnp.where(lane == SINK_LANE, sink_ref[2 * p], sink_ref[2 * p + 1])
            sm2 = jnp.where(ctx["meta_any"], sm, jnp.where(ctx["sink_lanes"], sink_pair, NEG))
            mx2 = jnp.where(ctx["first_head_lanes"], mxs[0], mxs[1])
            pbuf[rows, META_COL:META_COL + LANES] = jnp.exp2(sm2 - mx2).astype(BF16)

        def pv(c, j=j, r0=r0):
            o4 = _dot(p_ref[j * KV_HEADS + c], keys_of(vv_ref, c, j))
            for pi in range(PAIRS_PER_KV):
                p = c * PAIRS_PER_KV + pi
                rows = slice(pi * ATT_BLOCK, (pi + 1) * ATT_BLOCK)
                gate = _silu(z_ref[r0:r0 + ATT_BLOCK, OFF_AG + p * LANES:OFF_AG + (p + 1) * LANES])
                yw_ref[r0:r0 + ATT_BLOCK, p * LANES:(p + 1) * LANES] = (
                    o4[rows, 0:LANES] * (1.0 / o4[rows, LANES:2 * LANES]) * gate).astype(BF16)

        attn_units.append((kv_update, 1))
        for c in range(KV_HEADS):
            attn_units.append((functools.partial(q_scores, c), 0))
            attn_units.extend((functools.partial(softmax, c, pi), 1) for pi in range(PAIRS_PER_KV))
            attn_units.append((functools.partial(pv, c), 0))

    def kv_carry():
        for ref in (kk_ref, vv_ref):
            for c in range(KV_HEADS):
                for base in (0, BOT_ROW):
                    ref[c, base:base + ATT_BLOCK, 0:LANES] = (
                        ref[c, base + NBLK * ATT_BLOCK:base + (NBLK + 1) * ATT_BLOCK, 0:LANES])

    crow = lax.broadcasted_iota(jnp.int32, (CHUNK, LANES), 0)
    tril = (lax.broadcasted_iota(jnp.int32, (CHUNK, CHUNK), 0)
            >= lax.broadcasted_iota(jnp.int32, (CHUNK, CHUNK), 1))
    for cj in range(nchunk):
        q0 = cj * CHUNK
        g = {}

        def chunk_prep(cj=cj, q0=q0, g=g):
            if cj == 0:
                u_ref[0:SUBLANES, :] = u_ref[tb:tb + SUBLANES, :]
                u_ref[SUBLANES:SUBLANES + tb, :] = z_ref[:, OFF_MQ:OFF_MQ + QK_CONV_COLS]
            real =(i * tb + q0 + crow) >= PAD
            real_col = (i * tb + q0 + lax.broadcasted_iota(jnp.int32, (CHUNK, 1), 0)) >= PAD
            uu = u_ref[q0:q0 + SUBLANES + CHUNK, :]
            acc = cw_ref[0:1, :] * uu
            for t in range(1, CONV_K):
                acc = cw_ref[t:t + 1, :] * uu + pltpu.roll(acc, 1, axis=0)
            conv = acc[SUBLANES:, :] + cb_ref[...]
            g["qk"] = jnp.where(real_col, _silu(conv), 0.0)

            li = jnp.where(real, z_ref[q0:q0 + CHUNK, OFF_GI:OFF_GI + LANES] + bi_ref[...], NEG)
            lf = jnp.where(real, _log_sigmoid(z_ref[q0:q0 + CHUNK, OFF_GF:OFF_GF + LANES] + bf_ref[...]), 0.0)
            b = _row_scan(lf, crow, jnp.add, 0.0)
            r = li - b
            rmax = jnp.max(r, axis=0, keepdims=True)
            g["wg"] = jnp.exp(r - rmax)
            dmax = b + _row_scan(r, crow, jnp.maximum, NEG)
            m_in = m_ref[0:1, :]
            a_t = b + m_in
            m_t = jnp.maximum(a_t, dmax)
            g["inter"] = jnp.exp(a_t - m_t)
            g["cvec"] = b - m_t
            g["em_t"] = jnp.exp(-m_t)
            g["r_t"] = r.T
            b_last = b[CHUNK - 1:CHUNK, :]
            mg = b_last + rmax
            m_new = jnp.maximum(b_last + m_in, mg)
            g["a_st"] = jnp.exp(b_last + m_in - m_new)
            g["w_st"] = jnp.exp(mg - m_new)
            m_ref[...] = jnp.broadcast_to(m_new, (SUBLANES, LANES))

        def heads_state(q0=q0, g=g):
            for hd in range(M_HEADS):
                qk = g["qk"]
                q = qk[:, hd * M_QK_DIM:(hd + 1) * M_QK_DIM]
                k = qk[:, (M_HEADS + hd) * M_QK_DIM:(M_HEADS + hd + 1) * M_QK_DIM] * (M_QK_DIM ** -0.5)
                vb = z_ref[q0:q0 + CHUNK, OFF_MV + hd * M_V_DIM:OFF_MV + (hd + 1) * M_V_DIM].astype(BF16)
                qb = q.astype(BF16)
                c_in = c_ref[hd]
                n_in = n_ref[hd, 0:1, :]
                g["s", hd] = _dot_nt(qb, k.astype(BF16))
                g["qc", hd] = _dot(qb, c_in.astype(BF16))
                g["qn", hd] = jnp.sum(q * n_in, axis=-1, keepdims=True)
                g["vb", hd] = vb

                kw = g["wg"][:, hd:hd + 1] * k
                a_s = g["a_st"][:, hd:hd + 1]
                w_s = g["w_st"][:, hd:hd + 1]
                c_ref[hd] = a_s * c_in + w_s * _dot_tn(kw.astype(BF16), vb)
                n_new = a_s * n_in + w_s * jnp.sum(kw, axis=0, keepdims=True)
                n_ref[hd] = jnp.broadcast_to(n_new, (SUBLANES, LANES))

        def head_out(hd, q0=q0, g=g):
            arg = g["cvec"][:, hd:hd + 1] + g["r_t"][hd:hd + 1, :]
            sg = g["s", hd] * jnp.exp(jnp.where(tril, arg, NEG))
            inter_c = g["inter"][:, hd:hd + 1]
            num = inter_c * g["qc", hd] + _dot(sg.astype(BF16), g["vb", hd])
            den = inter_c * g["qn", hd] + jnp.sum(sg, axis=-1, keepdims=True)
            hout = num / jnp.maximum(jnp.abs(den), g["em_t"][:, hd:hd + 1])

            cols = slice(hd * M_V_DIM, (hd + 1) * M_V_DIM)
            hm = hout * lax.rsqrt(jnp.mean(hout * hout, axis=-1, keepdims=True) + EPS) * og_ref[:, cols]
            ocols = slice(OFF_MO + hd * M_V_DIM, OFF_MO + (hd + 1) * M_V_DIM)
            gcols = slice(OFF_MG + hd * M_V_DIM, OFF_MG + (hd + 1) * M_V_DIM)
            ym = hm * _sigmoid(z_ref[q0:q0 + CHUNK, ocols]) * _silu(z_ref[q0:q0 + CHUNK, gcols])
            yw_ref[q0:q0 + CHUNK, D_MODEL + hd * M_V_DIM:D_MODEL + (hd + 1) * M_V_DIM] = ym.astype(BF16)

        mlstm_units.append((chunk_prep, 1))
        mlstm_units.append((heads_state, 0))
        mlstm_units.extend((functools.partial(head_out, hd), 1) for hd in range(M_HEADS))

    kv_carry()
    if len(attn_units) == len(UNIT_ORDER_A) and len(mlstm_units) == len(UNIT_ORDER_M):
        order = sorted([(pos, u) for pos, u in zip(UNIT_ORDER_A, attn_units)]
                       + [(pos, u) for pos, u in zip(UNIT_ORDER_M, mlstm_units)], key=lambda e: e[0])
        units = [u for _, u in order]
    else:
        units = _interleave(attn_units, mlstm_units)
    n_receivers = sum(takes for _, takes in units)
    n_spread = len(jobs) - TAIL_JOBS
    seen = 0
    for unit, takes in units:
        seen += takes
        while takes and len(jobs) > TAIL_JOBS and (n_spread + TAIL_JOBS - len(jobs)) * n_receivers < seen * n_spread:
            jobs.pop(0)()
        unit()
    for job in jobs:
        job()


def _layer_call(h, head, sink, rc, rs, ng, win, qg, kg, bd, cw, cb, bi, bf, og, wout, *, layer, first, last):
    batch, rows, d = h.shape
    tb = ROW_TILE
    nt = (rows + LEAD if first else rows) // tb
    last_tile = batch * nt - 1

    def tile_of(t):
        t = jnp.clip(t, 0, last_tile)
        return t // nt, t % nt, 0

    def token_tile_of(t):
        b, i, _ = tile_of(t)
        return b, jnp.maximum(i - LEAD // tb, 0), 0

    const = lambda s: (0, 0)
    in_tile = token_tile_of if first else tile_of
    mix_rows = lambda s: tile_of(s - 1)[1:]
    out_tile = token_tile_of if last else tile_of
    out_rows = rows + (LEAD if first else 0) - (LEAD if last else 0)
    once = pl.Buffered(1)
    full = lambda a: pl.BlockSpec(a.shape, const)
    in_specs = [
        pl.BlockSpec(memory_space=pltpu.SMEM),
        pl.BlockSpec((1, tb, d), lambda s: in_tile(s)),
        pl.BlockSpec((1, tb, d), lambda s: in_tile(s - 2)),
        full(head),
        pl.BlockSpec((tb, LANES), mix_rows),
        pl.BlockSpec((tb, LANES), mix_rows),
        full(ng),
        *[pl.BlockSpec((None,) + w.shape[1:], lambda s: (layer, 0, 0), pipeline_mode=once) for w in win],
        full(qg), full(kg), full(bd), full(cw), full(cb), full(bi), full(bf), full(og),
        pl.BlockSpec((None,) + wout.shape[1:], lambda s: (layer, 0, 0), pipeline_mode=once),
    ]
    scratch = [
        pltpu.VMEM((tb, IN_COLS), F32),
        pltpu.VMEM((tb, IN_COLS), F32),
        pltpu.VMEM((tb, MIX_COLS), BF16),
        pltpu.VMEM((tb, MIX_COLS), BF16),
        pltpu.VMEM((tb, D_MODEL), BF16),
        pltpu.VMEM((KV_HEADS, KEY_ROWS, LANES), BF16),
        pltpu.VMEM((KV_HEADS, KEY_ROWS, 2 * LANES), BF16),
        pltpu.VMEM((NBLK * KV_HEADS, PAIRS_PER_KV * ATT_BLOCK, KEY_COLS), BF16),
        pltpu.VMEM((tb + SUBLANES, QK_CONV_COLS), F32),
        pltpu.VMEM((M_HEADS, M_QK_DIM, M_V_DIM), F32),
        pltpu.VMEM((M_HEADS, SUBLANES, LANES), F32),
        pltpu.VMEM((SUBLANES, LANES), F32),
    ]
    return pl.pallas_call(
        functools.partial(_layer_kernel, nt=nt, n_tiles=batch * nt, first=first),
        out_shape=jax.ShapeDtypeStruct((batch, out_rows, d), h.dtype),
        grid=(batch * nt + 2,),
        in_specs=in_specs,
        out_specs=pl.BlockSpec((1, tb, d), lambda s: out_tile(s - 2)),
        scratch_shapes=scratch,
        compiler_params=pltpu.CompilerParams(
            dimension_semantics=("arbitrary",),
            vmem_limit_bytes=VMEM_LIMIT_BYTES),
        name="hybrid_layer",
    )(sink, h, h, head, rc, rs, ng, *win, qg, kg, bd, cw, cb, bi, bf, og, wout)


def _rope_tables(lp):
    pos = jnp.arange(lp, dtype=F32) - PAD
    inv_freq = ROPE_THETA ** (-jnp.arange(0, ROT_DIM, 2, dtype=F32) / ROT_DIM)
    ang = pos[:, None] * inv_freq[None, :]
    cos, sin = jnp.cos(ang), jnp.sin(ang)
    ones = jnp.ones((lp, HEAD_DIM - ROT_DIM), F32)
    zeros = jnp.zeros((lp, HEAD_DIM - ROT_DIM), F32)
    rc = jnp.concatenate([cos, cos, ones], axis=1)
    rs = jnp.concatenate([-sin, sin, zeros], axis=1)
    return jnp.tile(rc, (1, 2)), jnp.tile(rs, (1, 2))


def _pad_lanes(v):
    return jnp.pad(v.astype(F32), (0, LANES - v.shape[0]))[None, :]


def kernel(x, meta, norm_g, w_in, attn_q_norm_g, attn_k_norm_g, attn_sink, mlstm_conv_w,
           mlstm_conv_b, mlstm_b_i, mlstm_b_f, mlstm_out_norm_g, w_out):
    batch, seq, d = x.shape
    depth = w_in.shape[0]
    assert d == D_MODEL and seq % ROW_TILE == 0
    lp = seq + LEAD
    head = jnp.concatenate([jnp.zeros((PAD, d), x.dtype), meta.astype(x.dtype)], axis=0)
    h = x
    rc, rs = _rope_tables(lp)
    head_id = jnp.arange(2 * LANES) // HEAD_DIM
    ones_bd = (head_id[:, None] == head_id[None, :]).astype(BF16)

    gate_lo = OFF_MO
    n_gate = 2 * M_HEADS
    lane_pad = ((0, 0), (0, 0), (0, LANES - M_HEADS))
    gi = jnp.pad(w_in[:, :, gate_lo:gate_lo + M_HEADS], lane_pad)
    gf = jnp.pad(w_in[:, :, gate_lo + M_HEADS:gate_lo + n_gate], lane_pad)
    stride_pad = ((0, 0), (0, 0), (0, WOUT_PAD_COLS))
    win = (w_in[:, :, :gate_lo].astype(BF16),
           jnp.pad(w_in[:, :, gate_lo + n_gate:].astype(BF16), stride_pad),
           jnp.pad(jnp.concatenate([gi, gf], axis=2).astype(BF16), stride_pad))
    wout = jnp.pad(w_out.astype(BF16), ((0, 0), (0, 0), (0, WOUT_PAD_COLS)))
    for l in range(depth):
        h = _layer_call(
            h, head, attn_sink[l].astype(F32) * LOG2_E, rc, rs,
            norm_g[l].astype(F32)[None, :], win,
            (jnp.tile(attn_q_norm_g[l].astype(F32), 2) * (HEAD_DIM ** -0.5 * LOG2_E))[None, :],
            jnp.tile(attn_k_norm_g[l].astype(F32), 2)[None, :], ones_bd,
            mlstm_conv_w[l].astype(F32), mlstm_conv_b[l].astype(F32)[None, :],
            _pad_lanes(mlstm_b_i[l]), _pad_lanes(mlstm_b_f[l]),
            mlstm_out_norm_g[l].astype(F32)[None, :], wout,
            layer=l, first=(l == 0), last=(l == depth - 1))
    return h
```

```python
import functools

import jax
import jax.numpy as jnp
from jax import lax
from jax.experimental import pallas as pl
from jax.experimental.pallas import tpu as pltpu

D_MODEL = 1024
N_META = 16
HEAD_DIM = 64
ATT_HEADS = 16
KV_HEADS = 2
ATT_BLOCK = 128
ROT_DIM = 16
ROPE_THETA = 500000.0
M_HEADS = 4
M_V_DIM = 256
M_QK_DIM = 128
CHUNK = 64
CONV_K = 4
EPS = 1e-6
NEG = -1e30
LOG2_E = 1.4426950408889634
NEG_LOG2_E = -LOG2_E

LANES = 128
SUBLANES = 8
MXU_COLS = 256
HEAD_PAIRS = ATT_HEADS // 2
PAIRS_PER_KV = HEAD_PAIRS // KV_HEADS
SUB = ATT_BLOCK // 2
WIN = SUB + ATT_BLOCK
KEY_COLS = 2 * WIN + LANES
META_COL = 2 * WIN
SINK_LANE = 2 * N_META

ROW_TILE = 128
LEAD = ROW_TILE
PAD = LEAD - N_META
META_BLK = PAD // ATT_BLOCK
META_OFF = PAD % ATT_BLOCK
NBLK = ROW_TILE // ATT_BLOCK
BOT_ROW = (NBLK + 1) * ATT_BLOCK
META_ROW = 2 * BOT_ROW
KEY_ROWS = META_ROW + ATT_BLOCK

OFF_Q = 0
OFF_K = 1024
OFF_V = 1152
OFF_AG = 1280
OFF_MQ = 2304
OFF_MK = 2816
OFF_MV = 3328
OFF_MO = 4352
OFF_MG = 5376
OFF_GI = 6400
OFF_GF = 6528
IN_COLS = 6656
QK_CONV_COLS = 2 * M_HEADS * M_QK_DIM
MIX_COLS = 2 * D_MODEL

IN_JOB_COLS = MXU_COLS
OUT_JOB_COLS = MXU_COLS
TAIL_JOBS = 1
UNIT_ORDER_A = (0, 1, 3, 5, 7, 9, 12, 10, 14, 16, 18, 20, 22)
UNIT_ORDER_M = (2, 4, 6, 8, 11, 13, 15, 17, 19, 21, 23, 24)
WOUT_PAD_COLS = LANES
VMEM_LIMIT_BYTES = 56 * 1024 * 1024

F32 = jnp.float32
BF16 = jnp.bfloat16


def _sigmoid(x):
    return 1.0 / (1.0 + jnp.exp2(x * NEG_LOG2_E))


def _silu(x):
    return x * _sigmoid(x)


def _log_sigmoid(x):
    return jnp.minimum(x, 0.0) - jnp.log1p(jnp.exp(-jnp.abs(x)))


def _dot(a, b):
    return jnp.dot(a, b, preferred_element_type=F32)


def _dot_nt(a, b):
    return lax.dot_general(a, b, (((1,), (1,)), ((), ())), preferred_element_type=F32)


def _dot_tn(a, b):
    return lax.dot_general(a, b, (((0,), (0,)), ((), ())), preferred_element_type=F32)


def _row_scan(x, row, combine, fill):
    shift = 1
    while shift < x.shape[0]:
        x = combine(x, jnp.where(row >= shift, pltpu.roll(x, shift, axis=0), fill))
        shift *= 2
    return x


def _interleave(a, b):
    out, ia, ib = [], 0, 0
    while ia < len(a) or ib < len(b):
        if ib >= len(b) or (ia < len(a) and ia * len(b) <= ib * len(a)):
            out.append(a[ia]); ia += 1
        else:
            out.append(b[ib]); ib += 1
    return out


def _layer_kernel(sink_ref, hn_ref, h_ref, head_ref, rc_ref, rs_ref, ng_ref, wa_ref, wb_ref, wg_ref,
                  qg_ref, kg_ref, bd_ref,
                  cw_ref, cb_ref, bi_ref, bf_ref, og_ref, wout_ref, o_ref,
                  za_ref, zb_ref, ya_ref, yb_ref, hb_ref, kk_ref, vv_ref, p_ref, u_ref,
                  c_ref, n_ref, m_ref, *, nt, n_tiles, first):
    s = pl.program_id(0)

    @pl.when(s == 0)
    def _():
        zb_ref[...] = jnp.zeros_like(zb_ref)
        yb_ref[...] = jnp.zeros_like(yb_ref)

    @pl.when((s == 0) | (lax.rem(s - 1, nt) == 0))
    def _():
        kk_ref[...] = jnp.zeros_like(kk_ref)
        r = lax.broadcasted_iota(jnp.int32, (KEY_ROWS, LANES), 0)
        l = lax.broadcasted_iota(jnp.int32, (KEY_ROWS, LANES), 1)
        m = r - META_ROW
        top = (r < BOT_ROW) | ((m >= 0) & (m < N_META)) | (m == SINK_LANE)
        bot = ((r >= BOT_ROW) & (r < META_ROW)) | ((m >= N_META) & (m < SINK_LANE)) | (m == SINK_LANE + 1)
        ones = jnp.where(l < HEAD_DIM, jnp.where(top, 1.0, 0.0), jnp.where(bot, 1.0, 0.0)).astype(BF16)
        for c in range(KV_HEADS):
            vv_ref[c, :, 0:LANES] = jnp.zeros((KEY_ROWS, LANES), BF16)
            vv_ref[c, :, LANES:2 * LANES] = ones
        u_ref[ROW_TILE:ROW_TILE + SUBLANES, :] = jnp.zeros((SUBLANES, QK_CONV_COLS), F32)
        c_ref[...] = jnp.zeros_like(c_ref)
        n_ref[...] = jnp.zeros_like(n_ref)
        m_ref[...] = jnp.zeros_like(m_ref)

    step = functools.partial(
        _layer_step, sink_ref, hn_ref, h_ref, head_ref, rc_ref, rs_ref, ng_ref, (wa_ref, wb_ref, wg_ref),
        qg_ref, kg_ref, bd_ref,
        cw_ref, cb_ref, bi_ref, bf_ref, og_ref, wout_ref, o_ref,
        hb_ref, kk_ref, vv_ref, p_ref, u_ref, c_ref, n_ref, m_ref, nt=nt, n_tiles=n_tiles, first=first)

    @pl.when(s % 2 == 0)
    def _():
        step(za_ref, zb_ref, ya_ref, yb_ref)

    @pl.when(s % 2 == 1)
    def _():
        step(zb_ref, za_ref, yb_ref, ya_ref)


def _layer_step(sink_ref, hn_ref, h_ref, head_ref, rc_ref, rs_ref, ng_ref, win_refs, qg_ref, kg_ref, bd_ref,
                cw_ref, cb_ref, bi_ref, bf_ref, og_ref, wout_ref, o_ref,
                hb_ref, kk_ref, vv_ref, p_ref, u_ref, c_ref, n_ref, m_ref,
                zw_ref, z_ref, yw_ref, yr_ref, *, nt, n_tiles, first):
    tb = ROW_TILE
    s = pl.program_id(0)
    i = jnp.where(s < 1, -1, lax.rem(s - 1, nt))
    i_out = jnp.where(s < 2, -1, lax.rem(s - 2, nt))
    nchunk = tb // CHUNK

    hx = hn_ref[0]
    if first:
        hx = jnp.where(lax.rem(jnp.minimum(s, n_tiles - 1), nt) == 0, head_ref[...], hx)
    hb_ref[...] = (hx * lax.rsqrt(jnp.mean(hx * hx, axis=-1, keepdims=True) + EPS)
                   * ng_ref[...]).astype(BF16)
    orow = i_out * tb + lax.broadcasted_iota(jnp.int32, (tb, 1), 0)

    def in_proj_job(c0):
        ref, base = [(r, b) for r, b in zip(win_refs, (0, OFF_MO, OFF_GI)) if b <= c0][-1]

        def run():
            zw_ref[:, c0:c0 + IN_JOB_COLS] = _dot(hb_ref[...], ref[:, c0 - base:c0 - base + IN_JOB_COLS])
        return run

    def out_proj_job(c0):
        def run():
            y = (_dot(yr_ref[:, 0:D_MODEL], wout_ref[0:D_MODEL, c0:c0 + OUT_JOB_COLS])
                 + _dot(yr_ref[:, D_MODEL:MIX_COLS], wout_ref[D_MODEL:MIX_COLS, c0:c0 + OUT_JOB_COLS]))
            res = h_ref[0, :, c0:c0 + OUT_JOB_COLS]
            if first:
                res = jnp.where(i_out == 0, head_ref[:, c0:c0 + OUT_JOB_COLS], res)
            o_ref[0, :, c0:c0 + OUT_JOB_COLS] = jnp.where(orow >= PAD, res + y, 0.0)
        return run

    jobs = _interleave([in_proj_job(c0) for c0 in range(0, IN_COLS, IN_JOB_COLS)],
                       [out_proj_job(c0) for c0 in range(0, D_MODEL, OUT_JOB_COLS)])

    lane = lax.broadcasted_iota(jnp.int32, (ATT_BLOCK, LANES), 1)
    lo_half = lane < HEAD_DIM
    rot_lo = (lane % HEAD_DIM) < (ROT_DIM // 2)

    def head_rms_inv(x):
        w = x.shape[1]
        ss = _dot((x * x).astype(BF16), bd_ref[0:w, 0:w])
        return lax.rsqrt(ss * (1.0 / HEAD_DIM) + EPS)

    def rope(xn, rc, rs):
        partner = jnp.where(rot_lo, pltpu.roll(xn, LANES - ROT_DIM // 2, axis=1),
                            pltpu.roll(xn, ROT_DIM // 2, axis=1))
        return xn * rc + partner * rs

    def keys_of(ref, c, j, r):
        w0 = j * ATT_BLOCK + r * SUB
        return jnp.concatenate([ref[c, w0:w0 + WIN, :], ref[c, BOT_ROW + w0:BOT_ROW + w0 + WIN, :],
                                ref[c, META_ROW:META_ROW + ATT_BLOCK, :]], axis=0)

    attn_units, mlstm_units = [], []

    for j in range(NBLK):
        r0 = j * ATT_BLOCK
        blk = i * NBLK + j
        ctx = {}

        def kv_update(j=j, r0=r0, blk=blk, ctx=ctx):
            rc = rc_ref[r0:r0 + ATT_BLOCK, :]
            rs = rs_ref[r0:r0 + ATT_BLOCK, :]
            kx = z_ref[r0:r0 + ATT_BLOCK, OFF_K:OFF_K + LANES]
            kp = rope(kx * head_rms_inv(kx) * kg_ref[...], rc, rs)
            vp = z_ref[r0:r0 + ATT_BLOCK, OFF_V:OFF_V + LANES]
            kp_sw = pltpu.roll(kp, HEAD_DIM, axis=1)
            vp_sw = pltpu.roll(vp, HEAD_DIM, axis=1)
            slot = (j + 1) * ATT_BLOCK
            for c in range(KV_HEADS):
                for ref, own, swapped in ((kk_ref, kp, kp_sw), (vv_ref, vp, vp_sw)):
                    top = jnp.where(lo_half, own if c == 0 else swapped, 0.0).astype(BF16)
                    bot = jnp.where(lo_half, 0.0, swapped if c == 0 else own).astype(BF16)
                    ref[c, slot:slot + ATT_BLOCK, 0:LANES] = top
                    ref[c, BOT_ROW + slot:BOT_ROW + slot + ATT_BLOCK, 0:LANES] = bot
                    if j == META_BLK % NBLK:
                        for m0, part in ((META_ROW, top), (META_ROW + N_META, bot)):
                            keep = ref[c, m0:m0 + N_META, 0:LANES].astype(F32)
                            new = part[META_OFF:ATT_BLOCK, :].astype(F32)
                            ref[c, m0:m0 + N_META, 0:LANES] = jnp.where(blk == META_BLK, new, keep).astype(BF16)

            sl = lax.broadcasted_iota(jnp.int32, (SUB, LANES), 1)
            si = lax.broadcasted_iota(jnp.int32, (SUB, LANES), 0)
            for r in range(2):
                qrow = blk * ATT_BLOCK + r * SUB + si
                oks = []
                for w in (sl, jnp.where(sl < HEAD_DIM, ATT_BLOCK + sl, sl - HEAD_DIM), sl + HEAD_DIM):
                    krow = (blk - 1) * ATT_BLOCK + r * SUB + w
                    oks.append((krow >= LEAD) & (w > si) & (w <= si + ATT_BLOCK))
                ctx["band_ok", r] = oks
                meta_ok = (PAD + (sl % N_META)) <= qrow
                ctx["meta_ok", r] = ((sl < N_META) & meta_ok, (sl >= N_META) & (sl < 2 * N_META) & meta_ok)
                ctx["meta_any", r] = (sl < 2 * N_META) & meta_ok
            ctx["sink_lanes"] = (sl >> 1) == (SINK_LANE >> 1)
            ctx["first_head_lanes"] = (sl < N_META) | (sl == SINK_LANE)
            ctx["lane"] = sl

        def q_scores(c, j=j, r0=r0, ctx=ctx):
            rc = rc_ref[r0:r0 + ATT_BLOCK, :]
            rs = rs_ref[r0:r0 + ATT_BLOCK, :]
            gain2 = jnp.concatenate([qg_ref[...], qg_ref[...]], axis=1)
            q4 = []
            for half in range(PAIRS_PER_KV // 2):
                c0 = OFF_Q + (c * PAIRS_PER_KV + 2 * half) * LANES
                xq = z_ref[r0:r0 + ATT_BLOCK, c0:c0 + 2 * LANES]
                xn = xq * head_rms_inv(xq) * gain2
                q4.append(rope(xn[:, 0:LANES], rc, rs).astype(BF16))
                q4.append(rope(xn[:, LANES:2 * LANES], rc, rs).astype(BF16))
            for r in range(2):
                q4r = jnp.concatenate([t[r * SUB:(r + 1) * SUB, :] for t in q4], axis=0)
                ctx["s", c, r] = _dot_nt(q4r, keys_of(kk_ref, c, j, r))

        def softmax(c, pi, j=j, ctx=ctx):
            p = c * PAIRS_PER_KV + pi
            pbuf = p_ref.at[j * KV_HEADS + c]
            sl = ctx["lane"]
            lo = sl < HEAD_DIM
            sinks = (sink_ref[2 * p], sink_ref[2 * p + 1])
            sink_pair = jnp.where(sl == SINK_LANE, sinks[0], sinks[1])
            for r in range(2):
                sc = ctx["s", c, r][pi * SUB:(pi + 1) * SUB, :]
                rows = slice((r * PAIRS_PER_KV + pi) * SUB, (r * PAIRS_PER_KV + pi + 1) * SUB)
                b0, b1, b2 = (jnp.where(ok, sc[:, t * LANES:(t + 1) * LANES], NEG)
                              for t, ok in enumerate(ctx["band_ok", r]))
                sm = sc[:, META_COL:META_COL + LANES]
                sm_a = jnp.where(ctx["meta_ok", r][0], sm, NEG)
                sm_b = jnp.where(ctx["meta_ok", r][1], sm, NEG)
                both_a = jnp.maximum(jnp.maximum(b0, jnp.where(lo, b1, NEG)), sm_a)
                both_b = jnp.maximum(jnp.maximum(b2, jnp.where(lo, NEG, b1)), sm_b)
                mx_a = jnp.maximum(jnp.max(both_a, axis=-1, keepdims=True), sinks[0])
                mx_b = jnp.maximum(jnp.max(both_b, axis=-1, keepdims=True), sinks[1])
                pbuf[rows, 0:LANES] = jnp.exp2(b0 - mx_a).astype(BF16)
                pbuf[rows, LANES:2 * LANES] = jnp.exp2(b1 - jnp.where(lo, mx_a, mx_b)).astype(BF16)
                pbuf[rows, 2 * LANES:3 * LANES] = jnp.exp2(b2 - mx_b).astype(BF16)
                sm2 = jnp.where(ctx["meta_any", r], sm, jnp.where(ctx["sink_lanes"], sink_pair, NEG))
                mx2 = jnp.where(ctx["first_head_lanes"], mx_a, mx_b)
                pbuf[rows, META_COL:META_COL + LANES] = jnp.exp2(sm2 - mx2).astype(BF16)

        def pv(c, j=j, r0=r0):
            for r in range(2):
                o4 = _dot(p_ref[j * KV_HEADS + c, r * PAIRS_PER_KV * SUB:(r + 1) * PAIRS_PER_KV * SUB, :],
                          keys_of(vv_ref, c, j, r))
                y0 = r0 + r * SUB
                for pi in range(PAIRS_PER_KV):
                    p = c * PAIRS_PER_KV + pi
                    rows = slice(pi * SUB, (pi + 1) * SUB)
                    gate = _silu(z_ref[y0:y0 + SUB, OFF_AG + p * LANES:OFF_AG + (p + 1) * LANES])
                    yw_ref[y0:y0 + SUB, p * LANES:(p + 1) * LANES] = (
                        o4[rows, 0:LANES] * (1.0 / o4[rows, LANES:2 * LANES]) * gate).astype(BF16)

        attn_units.append((kv_update, 1))
        for c in range(KV_HEADS):
            attn_units.append((functools.partial(q_scores, c), 0))
            attn_units.extend((functools.partial(softmax, c, pi), 1) for pi in range(PAIRS_PER_KV))
            attn_units.append((functools.partial(pv, c), 0))

    def kv_carry():
        for ref in (kk_ref, vv_ref):
            for c in range(KV_HEADS):
                for base in (0, BOT_ROW):
                    ref[c, base:base + ATT_BLOCK, 0:LANES] = (
                        ref[c, base + NBLK * ATT_BLOCK:base + (NBLK + 1) * ATT_BLOCK, 0:LANES])

    crow = lax.broadcasted_iota(jnp.int32, (CHUNK, LANES), 0)
    tril = (lax.broadcasted_iota(jnp.int32, (CHUNK, CHUNK), 0)
            >= lax.broadcasted_iota(jnp.int32, (CHUNK, CHUNK), 1))
    for cj in range(nchunk):
        q0 = cj * CHUNK
        g = {}

        def chunk_prep(cj=cj, q0=q0, g=g):
            if cj == 0:
                u_ref[0:SUBLANES, :] = u_ref[tb:tb + SUBLANES, :]
                u_ref[SUBLANES:SUBLANES + tb, :] = z_ref[:, OFF_MQ:OFF_MQ + QK_CONV_COLS]
            real =(i * tb + q0 + crow) >= PAD
            real_col = (i * tb + q0 + lax.broadcasted_iota(jnp.int32, (CHUNK, 1), 0)) >= PAD
            uu = u_ref[q0:q0 + SUBLANES + CHUNK, :]
            acc = cw_ref[0:1, :] * uu
            for t in range(1, CONV_K):
                acc = cw_ref[t:t + 1, :] * uu + pltpu.roll(acc, 1, axis=0)
            conv = acc[SUBLANES:, :] + cb_ref[...]
            g["qk"] = jnp.where(real_col, _silu(conv), 0.0)

            li = jnp.where(real, z_ref[q0:q0 + CHUNK, OFF_GI:OFF_GI + LANES] + bi_ref[...], NEG)
            lf = jnp.where(real, _log_sigmoid(z_ref[q0:q0 + CHUNK, OFF_GF:OFF_GF + LANES] + bf_ref[...]), 0.0)
            b = _row_scan(lf, crow, jnp.add, 0.0)
            r = li - b
            rmax = jnp.max(r, axis=0, keepdims=True)
            g["wg"] = jnp.exp(r - rmax)
            dmax = b + _row_scan(r, crow, jnp.maximum, NEG)
            m_in = m_ref[0:1, :]
            a_t = b + m_in
            m_t = jnp.maximum(a_t, dmax)
            g["inter"] = jnp.exp(a_t - m_t)
            g["cvec"] = b - m_t
            g["em_t"] = jnp.exp(-m_t)
            g["r_t"] = r.T
            b_last = b[CHUNK - 1:CHUNK, :]
            mg = b_last + rmax
            m_new = jnp.maximum(b_last + m_in, mg)
            g["a_st"] = jnp.exp(b_last + m_in - m_new)
            g["w_st"] = jnp.exp(mg - m_new)
            m_ref[...] = jnp.broadcast_to(m_new, (SUBLANES, LANES))

        def heads_state(q0=q0, g=g):
            for hd in range(M_HEADS):
                qk = g["qk"]
                q = qk[:, hd * M_QK_DIM:(hd + 1) * M_QK_DIM]
                k = qk[:, (M_HEADS + hd) * M_QK_DIM:(M_HEADS + hd + 1) * M_QK_DIM] * (M_QK_DIM ** -0.5)
                vb = z_ref[q0:q0 + CHUNK, OFF_MV + hd * M_V_DIM:OFF_MV + (hd + 1) * M_V_DIM].astype(BF16)
                qb = q.astype(BF16)
                c_in = c_ref[hd]
                n_in = n_ref[hd, 0:1, :]
                g["s", hd] = _dot_nt(qb, k.astype(BF16))
                g["qc", hd] = _dot(qb, c_in.astype(BF16))
                g["qn", hd] = jnp.sum(q * n_in, axis=-1, keepdims=True)
                g["vb", hd] = vb

                kw = g["wg"][:, hd:hd + 1] * k
                a_s = g["a_st"][:, hd:hd + 1]
                w_s = g["w_st"][:, hd:hd + 1]
                c_ref[hd] = a_s * c_in + w_s * _dot_tn(kw.astype(BF16), vb)
                n_new = a_s * n_in + w_s * jnp.sum(kw, axis=0, keepdims=True)
                n_ref[hd] = jnp.broadcast_to(n_new, (SUBLANES, LANES))

        def head_out(hd, q0=q0, g=g):
            arg = g["cvec"][:, hd:hd + 1] + g["r_t"][hd:hd + 1, :]
            sg = g["s", hd] * jnp.exp(jnp.where(tril, arg, NEG))
            inter_c = g["inter"][:, hd:hd + 1]
            num = inter_c * g["qc", hd] + _dot(sg.astype(BF16), g["vb", hd])
            den = inter_c * g["qn", hd] + jnp.sum(sg, axis=-1, keepdims=True)
            hout = num / jnp.maximum(jnp.abs(den), g["em_t"][:, hd:hd + 1])

            cols = slice(hd * M_V_DIM, (hd + 1) * M_V_DIM)
            hm = hout * lax.rsqrt(jnp.mean(hout * hout, axis=-1, keepdims=True) + EPS) * og_ref[:, cols]
            ocols = slice(OFF_MO + hd * M_V_DIM, OFF_MO + (hd + 1) * M_V_DIM)
            gcols = slice(OFF_MG + hd * M_V_DIM, OFF_MG + (hd + 1) * M_V_DIM)
            ym = hm * _sigmoid(z_ref[q0:q0 + CHUNK, ocols]) * _silu(z_ref[q0:q0 + CHUNK, gcols])
            yw_ref[q0:q0 + CHUNK, D_MODEL + hd * M_V_DIM:D_MODEL + (hd + 1) * M_V_DIM] = ym.astype(BF16)

        mlstm_units.append((chunk_prep, 1))
        mlstm_units.append((heads_state, 0))
        mlstm_units.extend((functools.partial(head_out, hd), 1) for hd in range(M_HEADS))

    kv_carry()
    if len(attn_units) == len(UNIT_ORDER_A) and len(mlstm_units) == len(UNIT_ORDER_M):
        order = sorted([(pos, u) for pos, u in zip(UNIT_ORDER_A, attn_units)]
                       + [(pos, u) for pos, u in zip(UNIT_ORDER_M, mlstm_units)], key=lambda e: e[0])
        units = [u for _, u in order]
    else:
        units = _interleave(attn_units, mlstm_units)
    n_receivers = sum(takes for _, takes in units)
    n_spread = len(jobs) - TAIL_JOBS
    seen = 0
    for unit, takes in units:
        seen += takes
        while takes and len(jobs) > TAIL_JOBS and (n_spread + TAIL_JOBS - len(jobs)) * n_receivers < seen * n_spread:
            jobs.pop(0)()
        unit()
    for job in jobs:
        job()


def _layer_call(h, head, sink, rc, rs, ng, win, qg, kg, bd, cw, cb, bi, bf, og, wout, *, layer, first, last):
    batch, rows, d = h.shape
    tb = ROW_TILE
    nt = (rows + LEAD if first else rows) // tb
    last_tile = batch * nt - 1

    def tile_of(t):
        t = jnp.clip(t, 0, last_tile)
        return t // nt, t % nt, 0

    def token_tile_of(t):
        b, i, _ = tile_of(t)
        return b, jnp.maximum(i - LEAD // tb, 0), 0

    const = lambda s: (0, 0)
    in_tile = token_tile_of if first else tile_of
    mix_rows = lambda s: tile_of(s - 1)[1:]
    out_tile = token_tile_of if last else tile_of
    out_rows = rows + (LEAD if first else 0) - (LEAD if last else 0)
    once = pl.Buffered(1)
    full = lambda a: pl.BlockSpec(a.shape, const)
    in_specs = [
        pl.BlockSpec(memory_space=pltpu.SMEM),
        pl.BlockSpec((1, tb, d), lambda s: in_tile(s)),
        pl.BlockSpec((1, tb, d), lambda s: in_tile(s - 2)),
        full(head),
        pl.BlockSpec((tb, LANES), mix_rows),
        pl.BlockSpec((tb, LANES), mix_rows),
        full(ng),
        *[pl.BlockSpec((None,) + w.shape[1:], lambda s: (layer, 0, 0), pipeline_mode=once) for w in win],
        full(qg), full(kg), full(bd), full(cw), full(cb), full(bi), full(bf), full(og),
        pl.BlockSpec((None,) + wout.shape[1:], lambda s: (layer, 0, 0), pipeline_mode=once),
    ]
    scratch = [
        pltpu.VMEM((tb, IN_COLS), F32),
        pltpu.VMEM((tb, IN_COLS), F32),
        pltpu.VMEM((tb, MIX_COLS), BF16),
        pltpu.VMEM((tb, MIX_COLS), BF16),
        pltpu.VMEM((tb, D_MODEL), BF16),
        pltpu.VMEM((KV_HEADS, KEY_ROWS, LANES), BF16),
        pltpu.VMEM((KV_HEADS, KEY_ROWS, 2 * LANES), BF16),
        pltpu.VMEM((NBLK * KV_HEADS, PAIRS_PER_KV * ATT_BLOCK, KEY_COLS), BF16),
        pltpu.VMEM((tb + SUBLANES, QK_CONV_COLS), F32),
        pltpu.VMEM((M_HEADS, M_QK_DIM, M_V_DIM), F32),
        pltpu.VMEM((M_HEADS, SUBLANES, LANES), F32),
        pltpu.VMEM((SUBLANES, LANES), F32),
    ]
    return pl.pallas_call(
        functools.partial(_layer_kernel, nt=nt, n_tiles=batch * nt, first=first),
        out_shape=jax.ShapeDtypeStruct((batch, out_rows, d), h.dtype),
        grid=(batch * nt + 2,),
        in_specs=in_specs,
        out_specs=pl.BlockSpec((1, tb, d), lambda s: out_tile(s - 2)),
        scratch_shapes=scratch,
        compiler_params=pltpu.CompilerParams(
            dimension_semantics=("arbitrary",),
            vmem_limit_bytes=VMEM_LIMIT_BYTES),
        name="hybrid_layer",
    )(sink, h, h, head, rc, rs, ng, *win, qg, kg, bd, cw, cb, bi, bf, og, wout)


def _rope_tables(lp):
    pos = jnp.arange(lp, dtype=F32) - PAD
    inv_freq = ROPE_THETA ** (-jnp.arange(0, ROT_DIM, 2, dtype=F32) / ROT_DIM)
    ang = pos[:, None] * inv_freq[None, :]
    cos, sin = jnp.cos(ang), jnp.sin(ang)
    ones = jnp.ones((lp, HEAD_DIM - ROT_DIM), F32)
    zeros = jnp.zeros((lp, HEAD_DIM - ROT_DIM), F32)
    rc = jnp.concatenate([cos, cos, ones], axis=1)
    rs = jnp.concatenate([-sin, sin, zeros], axis=1)
    return jnp.tile(rc, (1, 2)), jnp.tile(rs, (1, 2))


def _pad_lanes(v):
    return jnp.pad(v.astype(F32), (0, LANES - v.shape[0]))[None, :]


def kernel(x, meta, norm_g, w_in, attn_q_norm_g, attn_k_norm_g, attn_sink, mlstm_conv_w,
           mlstm_conv_b, mlstm_b_i, mlstm_b_f, mlstm_out_norm_g, w_out):
    batch, seq, d = x.shape
    depth = w_in.shape[0]
    assert d == D_MODEL and seq % ROW_TILE == 0
    lp = seq + LEAD
    head = jnp.concatenate([jnp.zeros((PAD, d), x.dtype), meta.astype(x.dtype)], axis=0)
    h = x
    rc, rs = _rope_tables(lp)
    head_id = jnp.arange(2 * LANES) // HEAD_DIM
    ones_bd = (head_id[:, None] == head_id[None, :]).astype(BF16)

    gate_lo = OFF_MO
    n_gate = 2 * M_HEADS
    lane_pad = ((0, 0), (0, 0), (0, LANES - M_HEADS))
    gi = jnp.pad(w_in[:, :, gate_lo:gate_lo + M_HEADS], lane_pad)
    gf = jnp.pad(w_in[:, :, gate_lo + M_HEADS:gate_lo + n_gate], lane_pad)
    stride_pad = ((0, 0), (0, 0), (0, WOUT_PAD_COLS))
    win = (w_in[:, :, :gate_lo].astype(BF16),
           jnp.pad(w_in[:, :, gate_lo + n_gate:].astype(BF16), stride_pad),
           jnp.pad(jnp.concatenate([gi, gf], axis=2).astype(BF16), stride_pad))
    wout = jnp.pad(w_out.astype(BF16), ((0, 0), (0, 0), (0, WOUT_PAD_COLS)))
    for l in range(depth):
        h = _layer_call(
            h, head, attn_sink[l].astype(F32) * LOG2_E, rc, rs,
            norm_g[l].astype(F32)[None, :], win,
            (jnp.tile(attn_q_norm_g[l].astype(F32), 2) * (HEAD_DIM ** -0.5 * LOG2_E))[None, :],
            jnp.tile(attn_k_norm_g[l].astype(F32), 2)[None, :], ones_bd,
            mlstm_conv_w[l].astype(F32), mlstm_conv_b[l].astype(F32)[None, :],
            _pad_lanes(mlstm_b_i[l]), _pad_lanes(mlstm_b_f[l]),
            mlstm_out_norm_g[l].astype(F32)[None, :], wout,
            layer=l, first=(l == 0), last=(l == depth - 1))
    return h
```

```python
import functools

import jax
import jax.numpy as jnp
from jax import lax
from jax.experimental import pallas as pl
from jax.experimental.pallas import tpu as pltpu

D_MODEL = 1024
N_META = 16
HEAD_DIM = 64
ATT_HEADS = 16
KV_HEADS = 2
ATT_BLOCK = 128
ROT_DIM = 16
ROPE_THETA = 500000.0
M_HEADS = 4
M_V_DIM = 256
M_QK_DIM = 128
CHUNK = 64
CONV_K = 4
EPS = 1e-6
NEG = -1e30
LOG2_E = 1.4426950408889634
NEG_LOG2_E = -LOG2_E

LANES = 128
SUBLANES = 8
MXU_COLS = 256
HEAD_PAIRS = ATT_HEADS // 2
PAIRS_PER_KV = HEAD_PAIRS // KV_HEADS
SUB = ATT_BLOCK // 2
WIN = SUB + ATT_BLOCK
KEY_COLS = 2 * WIN + LANES
META_COL = 2 * WIN
SINK_LANE = 2 * N_META

ROW_TILE = 128
LEAD = ROW_TILE
PAD = LEAD - N_META
META_BLK = PAD // ATT_BLOCK
META_OFF = PAD % ATT_BLOCK
NBLK = ROW_TILE // ATT_BLOCK
BOT_ROW = (NBLK + 1) * ATT_BLOCK
META_ROW = 2 * BOT_ROW
KEY_ROWS = META_ROW + ATT_BLOCK

OFF_Q = 0
OFF_K = 1024
OFF_V = 1152
OFF_AG = 1280
OFF_MQ = 2304
OFF_MK = 2816
OFF_MV = 3328
OFF_MO = 4352
OFF_MG = 5376
OFF_GI = 6400
OFF_GF = 6528
IN_COLS = 6656
QK_CONV_COLS = 2 * M_HEADS * M_QK_DIM
MIX_COLS = 2 * D_MODEL

IN_JOB_COLS = MXU_COLS
OUT_JOB_COLS = MXU_COLS
TAIL_JOBS = 0
UNIT_ORDER_A = (0, 1, 3, 5, 7, 9, 12, 10, 14, 16, 18, 20, 22)
UNIT_ORDER_M = (2, 4, 6, 8, 11, 13, 15, 17, 19, 21, 23, 24)
WOUT_PAD_COLS = LANES
VMEM_LIMIT_BYTES = 56 * 1024 * 1024

F32 = jnp.float32
BF16 = jnp.bfloat16


def _sigmoid(x):
    return 1.0 / (1.0 + jnp.exp2(x * NEG_LOG2_E))


def _silu(x):
    return x * _sigmoid(x)


def _log_sigmoid(x):
    return jnp.minimum(x, 0.0) - jnp.log1p(jnp.exp(-jnp.abs(x)))


def _dot(a, b):
    return jnp.dot(a, b, preferred_element_type=F32)


def _dot_nt(a, b):
    return lax.dot_general(a, b, (((1,), (1,)), ((), ())), preferred_element_type=F32)


def _dot_tn(a, b):
    return lax.dot_general(a, b, (((0,), (0,)), ((), ())), preferred_element_type=F32)


def _row_scan(x, row, combine, fill):
    shift = 1
    while shift < x.shape[0]:
        x = combine(x, jnp.where(row >= shift, pltpu.roll(x, shift, axis=0), fill))
        shift *= 2
    return x


def _interleave(a, b):
    out, ia, ib = [], 0, 0
    while ia < len(a) or ib < len(b):
        if ib >= len(b) or (ia < len(a) and ia * len(b) <= ib * len(a)):
            out.append(a[ia]); ia += 1
        else:
            out.append(b[ib]); ib += 1
    return out


def _layer_kernel(sink_ref, hn_ref, h_ref, head_ref, rc_ref, rs_ref, ng_ref, wa_ref, wb_ref, wg_ref,
                  qg_ref, kg_ref, bd_ref,
                  cw_ref, cb_ref, bi_ref, bf_ref, og_ref, wout_ref, o_ref,
                  za_ref, zb_ref, ya_ref, yb_ref, ha_ref, hb_ref, kk_ref, vv_ref, p_ref, u_ref,
                  c_ref, n_ref, m_ref, *, nt, n_tiles, first):
    s = pl.program_id(0)

    @pl.when(s == 0)
    def _():
        zb_ref[...] = jnp.zeros_like(zb_ref)
        yb_ref[...] = jnp.zeros_like(yb_ref)
        hb_ref[...] = jnp.zeros_like(hb_ref)

    @pl.when((s == 0) | (lax.rem(s - 2, nt) == 0))
    def _():
        kk_ref[...] = jnp.zeros_like(kk_ref)
        r = lax.broadcasted_iota(jnp.int32, (KEY_ROWS, LANES), 0)
        l = lax.broadcasted_iota(jnp.int32, (KEY_ROWS, LANES), 1)
        m = r - META_ROW
        top = (r < BOT_ROW) | ((m >= 0) & (m < N_META)) | (m == SINK_LANE)
        bot = ((r >= BOT_ROW) & (r < META_ROW)) | ((m >= N_META) & (m < SINK_LANE)) | (m == SINK_LANE + 1)
        ones = jnp.where(l < HEAD_DIM, jnp.where(top, 1.0, 0.0), jnp.where(bot, 1.0, 0.0)).astype(BF16)
        for c in range(KV_HEADS):
            vv_ref[c, :, 0:LANES] = jnp.zeros((KEY_ROWS, LANES), BF16)
            vv_ref[c, :, LANES:2 * LANES] = ones
        u_ref[ROW_TILE:ROW_TILE + SUBLANES, :] = jnp.zeros((SUBLANES, QK_CONV_COLS), F32)
        c_ref[...] = jnp.zeros_like(c_ref)
        n_ref[...] = jnp.zeros_like(n_ref)
        m_ref[...] = jnp.zeros_like(m_ref)

    step = functools.partial(
        _layer_step, sink_ref, hn_ref, h_ref, head_ref, rc_ref, rs_ref, ng_ref, (wa_ref, wb_ref, wg_ref),
        qg_ref, kg_ref, bd_ref,
        cw_ref, cb_ref, bi_ref, bf_ref, og_ref, wout_ref, o_ref,
        kk_ref, vv_ref, p_ref, u_ref, c_ref, n_ref, m_ref, nt=nt, n_tiles=n_tiles, first=first)

    @pl.when(s % 2 == 0)
    def _():
        step(za_ref, zb_ref, ya_ref, yb_ref, ha_ref, hb_ref)

    @pl.when(s % 2 == 1)
    def _():
        step(zb_ref, za_ref, yb_ref, ya_ref, hb_ref, ha_ref)


def _layer_step(sink_ref, hn_ref, h_ref, head_ref, rc_ref, rs_ref, ng_ref, win_refs, qg_ref, kg_ref, bd_ref,
                cw_ref, cb_ref, bi_ref, bf_ref, og_ref, wout_ref, o_ref,
                kk_ref, vv_ref, p_ref, u_ref, c_ref, n_ref, m_ref,
                zw_ref, z_ref, yw_ref, yr_ref, hw_ref, hr_ref, *, nt, n_tiles, first):
    tb = ROW_TILE
    s = pl.program_id(0)
    i = jnp.where(s < 2, -1, lax.rem(s - 2, nt))
    i_out = jnp.where(s < 3, -1, lax.rem(s - 3, nt))
    nchunk = tb // CHUNK

    hx = hn_ref[0]
    if first:
        hx = jnp.where(lax.rem(jnp.minimum(s, n_tiles - 1), nt) == 0, head_ref[...], hx)
    hw_ref[...] = (hx * lax.rsqrt(jnp.mean(hx * hx, axis=-1, keepdims=True) + EPS)
                   * ng_ref[...]).astype(BF16)
    orow = i_out * tb + lax.broadcasted_iota(jnp.int32, (tb, 1), 0)

    def in_proj_job(c0):
        ref, base = [(r, b) for r, b in zip(win_refs, (0, OFF_MO, OFF_GI)) if b <= c0][-1]

        def run():
            zw_ref[:, c0:c0 + IN_JOB_COLS] = _dot(hr_ref[...], ref[:, c0 - base:c0 - base + IN_JOB_COLS])
        return run

    def out_proj_job(c0):
        def run():
            y = (_dot(yr_ref[:, 0:D_MODEL], wout_ref[0:D_MODEL, c0:c0 + OUT_JOB_COLS])
                 + _dot(yr_ref[:, D_MODEL:MIX_COLS], wout_ref[D_MODEL:MIX_COLS, c0:c0 + OUT_JOB_COLS]))
            res = h_ref[0, :, c0:c0 + OUT_JOB_COLS]
            if first:
                res = jnp.where(i_out == 0, head_ref[:, c0:c0 + OUT_JOB_COLS], res)
            o_ref[0, :, c0:c0 + OUT_JOB_COLS] = jnp.where(orow >= PAD, res + y, 0.0)
        return run

    jobs = _interleave([in_proj_job(c0) for c0 in range(0, IN_COLS, IN_JOB_COLS)],
                       [out_proj_job(c0) for c0 in range(0, D_MODEL, OUT_JOB_COLS)])

    lane = lax.broadcasted_iota(jnp.int32, (ATT_BLOCK, LANES), 1)
    lo_half = lane < HEAD_DIM
    rot_lo = (lane % HEAD_DIM) < (ROT_DIM // 2)

    def head_rms_inv(x):
        w = x.shape[1]
        ss = _dot((x * x).astype(BF16), bd_ref[0:w, 0:w])
        return lax.rsqrt(ss * (1.0 / HEAD_DIM) + EPS)

    def rope(xn, rc, rs):
        partner = jnp.where(rot_lo, pltpu.roll(xn, LANES - ROT_DIM // 2, axis=1),
                            pltpu.roll(xn, ROT_DIM // 2, axis=1))
        return xn * rc + partner * rs

    def keys_of(ref, c, j, r):
        w0 = j * ATT_BLOCK + r * SUB
        return jnp.concatenate([ref[c, w0:w0 + WIN, :], ref[c, BOT_ROW + w0:BOT_ROW + w0 + WIN, :],
                                ref[c, META_ROW:META_ROW + ATT_BLOCK, :]], axis=0)

    attn_units, mlstm_units = [], []

    for j in range(NBLK):
        r0 = j * ATT_BLOCK
        blk = i * NBLK + j
        ctx = {}

        def kv_update(j=j, r0=r0, blk=blk, ctx=ctx):
            rc = rc_ref[r0:r0 + ATT_BLOCK, :]
            rs = rs_ref[r0:r0 + ATT_BLOCK, :]
            kx = z_ref[r0:r0 + ATT_BLOCK, OFF_K:OFF_K + LANES]
            kp = rope(kx * head_rms_inv(kx) * kg_ref[...], rc, rs)
            vp = z_ref[r0:r0 + ATT_BLOCK, OFF_V:OFF_V + LANES]
            kp_sw = pltpu.roll(kp, HEAD_DIM, axis=1)
            vp_sw = pltpu.roll(vp, HEAD_DIM, axis=1)
            slot = (j + 1) * ATT_BLOCK
            for c in range(KV_HEADS):
                for ref, own, swapped in ((kk_ref, kp, kp_sw), (vv_ref, vp, vp_sw)):
                    top = jnp.where(lo_half, own if c == 0 else swapped, 0.0).astype(BF16)
                    bot = jnp.where(lo_half, 0.0, swapped if c == 0 else own).astype(BF16)
                    ref[c, slot:slot + ATT_BLOCK, 0:LANES] = top
                    ref[c, BOT_ROW + slot:BOT_ROW + slot + ATT_BLOCK, 0:LANES] = bot
                    if j == META_BLK % NBLK:
                        for m0, part in ((META_ROW, top), (META_ROW + N_META, bot)):
                            keep = ref[c, m0:m0 + N_META, 0:LANES].astype(F32)
                            new = part[META_OFF:ATT_BLOCK, :].astype(F32)
                            ref[c, m0:m0 + N_META, 0:LANES] = jnp.where(blk == META_BLK, new, keep).astype(BF16)

            sl = lax.broadcasted_iota(jnp.int32, (SUB, LANES), 1)
            si = lax.broadcasted_iota(jnp.int32, (SUB, LANES), 0)
            for r in range(2):
                qrow = blk * ATT_BLOCK + r * SUB + si
                oks = []
                for w in (sl, jnp.where(sl < HEAD_DIM, ATT_BLOCK + sl, sl - HEAD_DIM), sl + HEAD_DIM):
                    krow = (blk - 1) * ATT_BLOCK + r * SUB + w
                    oks.append((krow >= LEAD) & (w > si) & (w <= si + ATT_BLOCK))
                ctx["band_ok", r] = oks
                meta_ok = (PAD + (sl % N_META)) <= qrow
                ctx["meta_ok", r] = ((sl < N_META) & meta_ok, (sl >= N_META) & (sl < 2 * N_META) & meta_ok)
                ctx["meta_any", r] = (sl < 2 * N_META) & meta_ok
            ctx["sink_lanes"] = (sl >> 1) == (SINK_LANE >> 1)
            ctx["first_head_lanes"] = (sl < N_META) | (sl == SINK_LANE)
            ctx["lane"] = sl

        def q_scores(c, j=j, r0=r0, ctx=ctx):
            rc = rc_ref[r0:r0 + ATT_BLOCK, :]
            rs = rs_ref[r0:r0 + ATT_BLOCK, :]
            gain2 = jnp.concatenate([qg_ref[...], qg_ref[...]], axis=1)
            q4 = []
            for half in range(PAIRS_PER_KV // 2):
                c0 = OFF_Q + (c * PAIRS_PER_KV + 2 * half) * LANES
                xq = z_ref[r0:r0 + ATT_BLOCK, c0:c0 + 2 * LANES]
                xn = xq * head_rms_inv(xq) * gain2
                q4.append(rope(xn[:, 0:LANES], rc, rs).astype(BF16))
                q4.append(rope(xn[:, LANES:2 * LANES], rc, rs).astype(BF16))
            for r in range(2):
                q4r = jnp.concatenate([t[r * SUB:(r + 1) * SUB, :] for t in q4], axis=0)
                ctx["s", c, r] = _dot_nt(q4r, keys_of(kk_ref, c, j, r))

        def softmax(c, pi, j=j, ctx=ctx):
            p = c * PAIRS_PER_KV + pi
            pbuf = p_ref.at[j * KV_HEADS + c]
            sl = ctx["lane"]
            lo = sl < HEAD_DIM
            sinks = (sink_ref[2 * p], sink_ref[2 * p + 1])
            sink_pair = jnp.where(sl == SINK_LANE, sinks[0], sinks[1])
            for r in range(2):
                sc = ctx["s", c, r][pi * SUB:(pi + 1) * SUB, :]
                rows = slice((r * PAIRS_PER_KV + pi) * SUB, (r * PAIRS_PER_KV + pi + 1) * SUB)
                b0, b1, b2 = (jnp.where(ok, sc[:, t * LANES:(t + 1) * LANES], NEG)
                              for t, ok in enumerate(ctx["band_ok", r]))
                sm = sc[:, META_COL:META_COL + LANES]
                sm_a = jnp.where(ctx["meta_ok", r][0], sm, NEG)
                sm_b = jnp.where(ctx["meta_ok", r][1], sm, NEG)
                both_a = jnp.maximum(jnp.maximum(b0, jnp.where(lo, b1, NEG)), sm_a)
                both_b = jnp.maximum(jnp.maximum(b2, jnp.where(lo, NEG, b1)), sm_b)
                mx_a = jnp.maximum(jnp.max(both_a, axis=-1, keepdims=True), sinks[0])
                mx_b = jnp.maximum(jnp.max(both_b, axis=-1, keepdims=True), sinks[1])
                pbuf[rows, 0:LANES] = jnp.exp2(b0 - mx_a).astype(BF16)
                pbuf[rows, LANES:2 * LANES] = jnp.exp2(b1 - jnp.where(lo, mx_a, mx_b)).astype(BF16)
                pbuf[rows, 2 * LANES:3 * LANES] = jnp.exp2(b2 - mx_b).astype(BF16)
                sm2 = jnp.where(ctx["meta_any", r], sm, jnp.where(ctx["sink_lanes"], sink_pair, NEG))
                mx2 = jnp.where(ctx["first_head_lanes"], mx_a, mx_b)
                pbuf[rows, META_COL:META_COL + LANES] = jnp.exp2(sm2 - mx2).astype(BF16)

        def pv(c, j=j, r0=r0):
            for r in range(2):
                o4 = _dot(p_ref[j * KV_HEADS + c, r * PAIRS_PER_KV * SUB:(r + 1) * PAIRS_PER_KV * SUB, :],
                          keys_of(vv_ref, c, j, r))
                y0 = r0 + r * SUB
                for pi in range(PAIRS_PER_KV):
                    p = c * PAIRS_PER_KV + pi
                    rows = slice(pi * SUB, (pi + 1) * SUB)
                    gate = _silu(z_ref[y0:y0 + SUB, OFF_AG + p * LANES:OFF_AG + (p + 1) * LANES])
                    yw_ref[y0:y0 + SUB, p * LANES:(p + 1) * LANES] = (
                        o4[rows, 0:LANES] * (1.0 / o4[rows, LANES:2 * LANES]) * gate).astype(BF16)

        attn_units.append((kv_update, 1))
        for c in range(KV_HEADS):
            attn_units.append((functools.partial(q_scores, c), 0))
            attn_units.extend((functools.partial(softmax, c, pi), 1) for pi in range(PAIRS_PER_KV))
            attn_units.append((functools.partial(pv, c), 0))

    def kv_carry():
        for ref in (kk_ref, vv_ref):
            for c in range(KV_HEADS):
                for base in (0, BOT_ROW):
                    ref[c, base:base + ATT_BLOCK, 0:LANES] = (
                        ref[c, base + NBLK * ATT_BLOCK:base + (NBLK + 1) * ATT_BLOCK, 0:LANES])

    crow = lax.broadcasted_iota(jnp.int32, (CHUNK, LANES), 0)
    tril = (lax.broadcasted_iota(jnp.int32, (CHUNK, CHUNK), 0)
            >= lax.broadcasted_iota(jnp.int32, (CHUNK, CHUNK), 1))
    for cj in range(nchunk):
        q0 = cj * CHUNK
        g = {}

        def chunk_prep(cj=cj, q0=q0, g=g):
            if cj == 0:
                u_ref[0:SUBLANES, :] = u_ref[tb:tb + SUBLANES, :]
                u_ref[SUBLANES:SUBLANES + tb, :] = z_ref[:, OFF_MQ:OFF_MQ + QK_CONV_COLS]
            real =(i * tb + q0 + crow) >= PAD
            real_col = (i * tb + q0 + lax.broadcasted_iota(jnp.int32, (CHUNK, 1), 0)) >= PAD
            uu = u_ref[q0:q0 + SUBLANES + CHUNK, :]
            acc = cw_ref[0:1, :] * uu
            for t in range(1, CONV_K):
                acc = cw_ref[t:t + 1, :] * uu + pltpu.roll(acc, 1, axis=0)
            conv = acc[SUBLANES:, :] + cb_ref[...]
            g["qk"] = jnp.where(real_col, _silu(conv), 0.0)

            li = jnp.where(real, z_ref[q0:q0 + CHUNK, OFF_GI:OFF_GI + LANES] + bi_ref[...], NEG)
            lf = jnp.where(real, _log_sigmoid(z_ref[q0:q0 + CHUNK, OFF_GF:OFF_GF + LANES] + bf_ref[...]), 0.0)
            b = _row_scan(lf, crow, jnp.add, 0.0)
            r = li - b
            rmax = jnp.max(r, axis=0, keepdims=True)
            g["wg"] = jnp.exp(r - rmax)
            dmax = b + _row_scan(r, crow, jnp.maximum, NEG)
            m_in = m_ref[0:1, :]
            a_t = b + m_in
            m_t = jnp.maximum(a_t, dmax)
            g["inter"] = jnp.exp(a_t - m_t)
            g["cvec"] = b - m_t
            g["em_t"] = jnp.exp(-m_t)
            g["r_t"] = r.T
            b_last = b[CHUNK - 1:CHUNK, :]
            mg = b_last + rmax
            m_new = jnp.maximum(b_last + m_in, mg)
            g["a_st"] = jnp.exp(b_last + m_in - m_new)
            g["w_st"] = jnp.exp(mg - m_new)
            m_ref[...] = jnp.broadcast_to(m_new, (SUBLANES, LANES))

        def heads_state(q0=q0, g=g):
            for hd in range(M_HEADS):
                qk = g["qk"]
                q = qk[:, hd * M_QK_DIM:(hd + 1) * M_QK_DIM]
                k = qk[:, (M_HEADS + hd) * M_QK_DIM:(M_HEADS + hd + 1) * M_QK_DIM] * (M_QK_DIM ** -0.5)
                vb = z_ref[q0:q0 + CHUNK, OFF_MV + hd * M_V_DIM:OFF_MV + (hd + 1) * M_V_DIM].astype(BF16)
                qb = q.astype(BF16)
                c_in = c_ref[hd]
                n_in = n_ref[hd, 0:1, :]
                g["s", hd] = _dot_nt(qb, k.astype(BF16))
                g["qc", hd] = _dot(qb, c_in.astype(BF16))
                g["qn", hd] = jnp.sum(q * n_in, axis=-1, keepdims=True)
                g["vb", hd] = vb

                kw = g["wg"][:, hd:hd + 1] * k
                a_s = g["a_st"][:, hd:hd + 1]
                w_s = g["w_st"][:, hd:hd + 1]
                c_ref[hd] = a_s * c_in + w_s * _dot_tn(kw.astype(BF16), vb)
                n_new = a_s * n_in + w_s * jnp.sum(kw, axis=0, keepdims=True)
                n_ref[hd] = jnp.broadcast_to(n_new, (SUBLANES, LANES))

        def head_out(hd, q0=q0, g=g):
            arg = g["cvec"][:, hd:hd + 1] + g["r_t"][hd:hd + 1, :]
            sg = g["s", hd] * jnp.exp(jnp.where(tril, arg, NEG))
            inter_c = g["inter"][:, hd:hd + 1]
            num = inter_c * g["qc", hd] + _dot(sg.astype(BF16), g["vb", hd])
            den = inter_c * g["qn", hd] + jnp.sum(sg, axis=-1, keepdims=True)
            hout = num / jnp.maximum(jnp.abs(den), g["em_t"][:, hd:hd + 1])

            cols = slice(hd * M_V_DIM, (hd + 1) * M_V_DIM)
            hm = hout * lax.rsqrt(jnp.mean(hout * hout, axis=-1, keepdims=True) + EPS) * og_ref[:, cols]
            ocols = slice(OFF_MO + hd * M_V_DIM, OFF_MO + (hd + 1) * M_V_DIM)
            gcols = slice(OFF_MG + hd * M_V_DIM, OFF_MG + (hd + 1) * M_V_DIM)
            ym = hm * _sigmoid(z_ref[q0:q0 + CHUNK, ocols]) * _silu(z_ref[q0:q0 + CHUNK, gcols])
            yw_ref[q0:q0 + CHUNK, D_MODEL + hd * M_V_DIM:D_MODEL + (hd + 1) * M_V_DIM] = ym.astype(BF16)

        mlstm_units.append((chunk_prep, 1))
        mlstm_units.append((heads_state, 0))
        mlstm_units.extend((functools.partial(head_out, hd), 1) for hd in range(M_HEADS))

    kv_carry()
    if len(attn_units) == len(UNIT_ORDER_A) and len(mlstm_units) == len(UNIT_ORDER_M):
        order = sorted([(pos, u) for pos, u in zip(UNIT_ORDER_A, attn_units)]
                       + [(pos, u) for pos, u in zip(UNIT_ORDER_M, mlstm_units)], key=lambda e: e[0])
        units = [u for _, u in order]
    else:
        units = _interleave(attn_units, mlstm_units)
    n_receivers = sum(takes for _, takes in units)
    n_spread = len(jobs) - TAIL_JOBS
    seen = 0
    for unit, takes in units:
        seen += takes
        while takes and len(jobs) > TAIL_JOBS and (n_spread + TAIL_JOBS - len(jobs)) * n_receivers < seen * n_spread:
            jobs.pop(0)()
        unit()
    for job in jobs:
        job()


def _layer_call(h, head, sink, rc, rs, ng, win, qg, kg, bd, cw, cb, bi, bf, og, wout, *, layer, first, last):
    batch, rows, d = h.shape
    tb = ROW_TILE
    nt = (rows + LEAD if first else rows) // tb
    last_tile = batch * nt - 1

    def tile_of(t):
        t = jnp.clip(t, 0, last_tile)
        return t // nt, t % nt, 0

    def token_tile_of(t):
        b, i, _ = tile_of(t)
        return b, jnp.maximum(i - LEAD // tb, 0), 0

    const = lambda s: (0, 0)
    in_tile = token_tile_of if first else tile_of
    mix_rows = lambda s: tile_of(s - 2)[1:]
    out_tile = token_tile_of if last else tile_of
    out_rows = rows + (LEAD if first else 0) - (LEAD if last else 0)
    once = pl.Buffered(1)
    full = lambda a: pl.BlockSpec(a.shape, const)
    in_specs = [
        pl.BlockSpec(memory_space=pltpu.SMEM),
        pl.BlockSpec((1, tb, d), lambda s: in_tile(s)),
        pl.BlockSpec((1, tb, d), lambda s: in_tile(s - 3)),
        full(head),
        pl.BlockSpec((tb, LANES), mix_rows),
        pl.BlockSpec((tb, LANES), mix_rows),
        full(ng),
        *[pl.BlockSpec((None,) + w.shape[1:], lambda s: (layer, 0, 0), pipeline_mode=once) for w in win],
        full(qg), full(kg), full(bd), full(cw), full(cb), full(bi), full(bf), full(og),
        pl.BlockSpec((None,) + wout.shape[1:], lambda s: (layer, 0, 0), pipeline_mode=once),
    ]
    scratch = [
        pltpu.VMEM((tb, IN_COLS), F32),
        pltpu.VMEM((tb, IN_COLS), F32),
        pltpu.VMEM((tb, MIX_COLS), BF16),
        pltpu.VMEM((tb, MIX_COLS), BF16),
        pltpu.VMEM((tb, D_MODEL), BF16),
        pltpu.VMEM((tb, D_MODEL), BF16),
        pltpu.VMEM((KV_HEADS, KEY_ROWS, LANES), BF16),
        pltpu.VMEM((KV_HEADS, KEY_ROWS, 2 * LANES), BF16),
        pltpu.VMEM((NBLK * KV_HEADS, PAIRS_PER_KV * ATT_BLOCK, KEY_COLS), BF16),
        pltpu.VMEM((tb + SUBLANES, QK_CONV_COLS), F32),
        pltpu.VMEM((M_HEADS, M_QK_DIM, M_V_DIM), F32),
        pltpu.VMEM((M_HEADS, SUBLANES, LANES), F32),
        pltpu.VMEM((SUBLANES, LANES), F32),
    ]
    return pl.pallas_call(
        functools.partial(_layer_kernel, nt=nt, n_tiles=batch * nt, first=first),
        out_shape=jax.ShapeDtypeStruct((batch, out_rows, d), h.dtype),
        grid=(batch * nt + 3,),
        in_specs=in_specs,
        out_specs=pl.BlockSpec((1, tb, d), lambda s: out_tile(s - 3)),
        scratch_shapes=scratch,
        compiler_params=pltpu.CompilerParams(
            dimension_semantics=("arbitrary",),
            vmem_limit_bytes=VMEM_LIMIT_BYTES),
        name="hybrid_layer",
    )(sink, h, h, head, rc, rs, ng, *win, qg, kg, bd, cw, cb, bi, bf, og, wout)


def _rope_tables(lp):
    pos = jnp.arange(lp, dtype=F32) - PAD
    inv_freq = ROPE_THETA ** (-jnp.arange(0, ROT_DIM, 2, dtype=F32) / ROT_DIM)
    ang = pos[:, None] * inv_freq[None, :]
    cos, sin = jnp.cos(ang), jnp.sin(ang)
    ones = jnp.ones((lp, HEAD_DIM - ROT_DIM), F32)
    zeros = jnp.zeros((lp, HEAD_DIM - ROT_DIM), F32)
    rc = jnp.concatenate([cos, cos, ones], axis=1)
    rs = jnp.concatenate([-sin, sin, zeros], axis=1)
    return jnp.tile(rc, (1, 2)), jnp.tile(rs, (1, 2))


def _pad_lanes(v):
    return jnp.pad(v.astype(F32), (0, LANES - v.shape[0]))[None, :]


def kernel(x, meta, norm_g, w_in, attn_q_norm_g, attn_k_norm_g, attn_sink, mlstm_conv_w,
           mlstm_conv_b, mlstm_b_i, mlstm_b_f, mlstm_out_norm_g, w_out):
    batch, seq, d = x.shape
    depth = w_in.shape[0]
    assert d == D_MODEL and seq % ROW_TILE == 0
    lp = seq + LEAD
    head = jnp.concatenate([jnp.zeros((PAD, d), x.dtype), meta.astype(x.dtype)], axis=0)
    h = x
    rc, rs = _rope_tables(lp)
    head_id = jnp.arange(2 * LANES) // HEAD_DIM
    ones_bd = (head_id[:, None] == head_id[None, :]).astype(BF16)

    gate_lo = OFF_MO
    n_gate = 2 * M_HEADS
    lane_pad = ((0, 0), (0, 0), (0, LANES - M_HEADS))
    gi = jnp.pad(w_in[:, :, gate_lo:gate_lo + M_HEADS], lane_pad)
    gf = jnp.pad(w_in[:, :, gate_lo + M_HEADS:gate_lo + n_gate], lane_pad)
    stride_pad = ((0, 0), (0, 0), (0, WOUT_PAD_COLS))
    win = (w_in[:, :, :gate_lo].astype(BF16),
           jnp.pad(w_in[:, :, gate_lo + n_gate:].astype(BF16), stride_pad),
           jnp.pad(jnp.concatenate([gi, gf], axis=2).astype(BF16), stride_pad))
    wout = jnp.pad(w_out.astype(BF16), ((0, 0), (0, 0), (0, WOUT_PAD_COLS)))
    for l in range(depth):
        h = _layer_call(
            h, head, attn_sink[l].astype(F32) * LOG2_E, rc, rs,
            norm_g[l].astype(F32)[None, :], win,
            (jnp.tile(attn_q_norm_g[l].astype(F32), 2) * (HEAD_DIM ** -0.5 * LOG2_E))[None, :],
            jnp.tile(attn_k_norm_g[l].astype(F32), 2)[None, :], ones_bd,
            mlstm_conv_w[l].astype(F32), mlstm_conv_b[l].astype(F32)[None, :],
            _pad_lanes(mlstm_b_i[l]), _pad_lanes(mlstm_b_f[l]),
            mlstm_out_norm_g[l].astype(F32)[None, :], wout,
            layer=l, first=(l == 0), last=(l == depth - 1))
    return h
```

```python
import functools

import jax
import jax.numpy as jnp
from jax import lax
from jax.experimental import pallas as pl
from jax.experimental.pallas import tpu as pltpu

D_MODEL = 1024
N_META = 16
HEAD_DIM = 64
ATT_HEADS = 16
KV_HEADS = 2
ATT_BLOCK = 128
ROT_DIM = 16
ROPE_THETA = 500000.0
M_HEADS = 4
M_V_DIM = 256
M_QK_DIM = 128
CHUNK = 64
CONV_K = 4
EPS = 1e-6
NEG = -1e30
LOG2_E = 1.4426950408889634
NEG_LOG2_E = -LOG2_E

LANES = 128
SUBLANES = 8
MXU_COLS = 256
HEAD_PAIRS = ATT_HEADS // 2
PAIRS_PER_KV = HEAD_PAIRS // KV_HEADS
SUB = ATT_BLOCK // 2
WIN = SUB + ATT_BLOCK
KEY_COLS = 2 * WIN + LANES
META_COL = 2 * WIN
SINK_LANE = 2 * N_META

ROW_TILE = 128
LEAD = ROW_TILE
PAD = LEAD - N_META
META_BLK = PAD // ATT_BLOCK
META_OFF = PAD % ATT_BLOCK
NBLK = ROW_TILE // ATT_BLOCK
BOT_ROW = (NBLK + 1) * ATT_BLOCK
META_ROW = 2 * BOT_ROW
KEY_ROWS = META_ROW + ATT_BLOCK

OFF_Q = 0
OFF_K = 1024
OFF_V = 1152
OFF_AG = 1280
OFF_MQ = 2304
OFF_MK = 2816
OFF_MV = 3328
OFF_MO = 4352
OFF_MG = 5376
OFF_GI = 6400
OFF_GF = 6528
IN_COLS = 6656
QK_CONV_COLS = 2 * M_HEADS * M_QK_DIM
MIX_COLS = 2 * D_MODEL

IN_JOB_COLS = MXU_COLS
OUT_JOB_COLS = MXU_COLS
TAIL_JOBS = 0
UNIT_ORDER_A = (0, 1, 3, 5, 7, 9, 12, 10, 14, 16, 18, 20, 22)
UNIT_ORDER_M = (2, 4, 6, 8, 11, 13, 15, 17, 19, 21, 23, 24)
WOUT_PAD_COLS = LANES
VMEM_LIMIT_BYTES = 56 * 1024 * 1024

F32 = jnp.float32
BF16 = jnp.bfloat16


def _sigmoid(x):
    return 1.0 / (1.0 + jnp.exp2(x * NEG_LOG2_E))


def _silu(x):
    return x * _sigmoid(x)


def _log_sigmoid(x):
    return jnp.minimum(x, 0.0) - jnp.log1p(jnp.exp(-jnp.abs(x)))


def _dot(a, b):
    return jnp.dot(a, b, preferred_element_type=F32)


def _dot_nt(a, b):
    return lax.dot_general(a, b, (((1,), (1,)), ((), ())), preferred_element_type=F32)


def _dot_tn(a, b):
    return lax.dot_general(a, b, (((0,), (0,)), ((), ())), preferred_element_type=F32)


def _row_scan(x, row, combine, fill):
    shift = 1
    while shift < x.shape[0]:
        x = combine(x, jnp.where(row >= shift, pltpu.roll(x, shift, axis=0), fill))
        shift *= 2
    return x


def _interleave(a, b):
    out, ia, ib = [], 0, 0
    while ia < len(a) or ib < len(b):
        if ib >= len(b) or (ia < len(a) and ia * len(b) <= ib * len(a)):
            out.append(a[ia]); ia += 1
        else:
            out.append(b[ib]); ib += 1
    return out


def _layer_kernel(sink_ref, hn_ref, h_ref, head_ref, rc_ref, rs_ref, ng_ref, wa_ref, wb_ref, wg_ref,
                  qg_ref, kg_ref, bd_ref,
                  cw_ref, cb_ref, bi_ref, bf_ref, og_ref, wout_ref, o_ref,
                  za_ref, zb_ref, ya_ref, yb_ref, hb_ref, kk_ref, vv_ref, p_ref, u_ref,
                  c_ref, n_ref, m_ref, *, nt, n_tiles, first):
    s = pl.program_id(0)

    @pl.when(s == 0)
    def _():
        zb_ref[...] = jnp.zeros_like(zb_ref)
        yb_ref[...] = jnp.zeros_like(yb_ref)

    @pl.when((s == 0) | (lax.rem(s - 1, nt) == 0))
    def _():
        kk_ref[...] = jnp.zeros_like(kk_ref)
        r = lax.broadcasted_iota(jnp.int32, (KEY_ROWS, LANES), 0)
        l = lax.broadcasted_iota(jnp.int32, (KEY_ROWS, LANES), 1)
        m = r - META_ROW
        top = (r < BOT_ROW) | ((m >= 0) & (m < N_META)) | (m == SINK_LANE)
        bot = ((r >= BOT_ROW) & (r < META_ROW)) | ((m >= N_META) & (m < SINK_LANE)) | (m == SINK_LANE + 1)
        ones = jnp.where(l < HEAD_DIM, jnp.where(top, 1.0, 0.0), jnp.where(bot, 1.0, 0.0)).astype(BF16)
        for c in range(KV_HEADS):
            vv_ref[c, :, 0:LANES] = jnp.zeros((KEY_ROWS, LANES), BF16)
            vv_ref[c, :, LANES:2 * LANES] = ones
        u_ref[ROW_TILE:ROW_TILE + SUBLANES, :] = jnp.zeros((SUBLANES, QK_CONV_COLS), F32)
        c_ref[...] = jnp.zeros_like(c_ref)
        n_ref[...] = jnp.zeros_like(n_ref)
        m_ref[...] = jnp.zeros_like(m_ref)

    step = functools.partial(
        _layer_step, sink_ref, hn_ref, h_ref, head_ref, rc_ref, rs_ref, ng_ref, (wa_ref, wb_ref, wg_ref),
        qg_ref, kg_ref, bd_ref,
        cw_ref, cb_ref, bi_ref, bf_ref, og_ref, wout_ref, o_ref,
        hb_ref, kk_ref, vv_ref, p_ref, u_ref, c_ref, n_ref, m_ref, nt=nt, n_tiles=n_tiles, first=first)

    @pl.when(s % 2 == 0)
    def _():
        step(za_ref, zb_ref, ya_ref, yb_ref)

    @pl.when(s % 2 == 1)
    def _():
        step(zb_ref, za_ref, yb_ref, ya_ref)


def _layer_step(sink_ref, hn_ref, h_ref, head_ref, rc_ref, rs_ref, ng_ref, win_refs, qg_ref, kg_ref, bd_ref,
                cw_ref, cb_ref, bi_ref, bf_ref, og_ref, wout_ref, o_ref,
                hb_ref, kk_ref, vv_ref, p_ref, u_ref, c_ref, n_ref, m_ref,
                zw_ref, z_ref, yw_ref, yr_ref, *, nt, n_tiles, first):
    tb = ROW_TILE
    s = pl.program_id(0)
    i = jnp.where(s < 1, -1, lax.rem(s - 1, nt))
    i_out = jnp.where(s < 2, -1, lax.rem(s - 2, nt))
    nchunk = tb // CHUNK

    hx = hn_ref[0]
    if first:
        hx = jnp.where(lax.rem(jnp.minimum(s, n_tiles - 1), nt) == 0, head_ref[...], hx)
    hb_ref[...] = (hx * lax.rsqrt(jnp.mean(hx * hx, axis=-1, keepdims=True) + EPS)
                   * ng_ref[...]).astype(BF16)
    orow = i_out * tb + lax.broadcasted_iota(jnp.int32, (tb, 1), 0)

    def in_proj_job(c0):
        ref, base = [(r, b) for r, b in zip(win_refs, (0, OFF_MO, OFF_GI)) if b <= c0][-1]

        def run():
            zw_ref[:, c0:c0 + IN_JOB_COLS] = _dot(hb_ref[...], ref[:, c0 - base:c0 - base + IN_JOB_COLS])
        return run

    def out_proj_job(c0):
        def run():
            y = (_dot(yr_ref[:, 0:D_MODEL], wout_ref[0:D_MODEL, c0:c0 + OUT_JOB_COLS])
                 + _dot(yr_ref[:, D_MODEL:MIX_COLS], wout_ref[D_MODEL:MIX_COLS, c0:c0 + OUT_JOB_COLS]))
            res = h_ref[0, :, c0:c0 + OUT_JOB_COLS]
            if first:
                res = jnp.where(i_out == 0, head_ref[:, c0:c0 + OUT_JOB_COLS], res)
            o_ref[0, :, c0:c0 + OUT_JOB_COLS] = jnp.where(orow >= PAD, res + y, 0.0)
        return run

    jobs = _interleave([in_proj_job(c0) for c0 in range(0, IN_COLS, IN_JOB_COLS)],
                       [out_proj_job(c0) for c0 in range(0, D_MODEL, OUT_JOB_COLS)])

    lane = lax.broadcasted_iota(jnp.int32, (ATT_BLOCK, LANES), 1)
    lo_half = lane < HEAD_DIM
    rot_lo = (lane % HEAD_DIM) < (ROT_DIM // 2)

    def head_rms_inv(x):
        w = x.shape[1]
        ss = _dot((x * x).astype(BF16), bd_ref[0:w, 0:w])
        return lax.rsqrt(ss * (1.0 / HEAD_DIM) + EPS)

    def rope(xn, rc, rs):
        partner = jnp.where(rot_lo, pltpu.roll(xn, LANES - ROT_DIM // 2, axis=1),
                            pltpu.roll(xn, ROT_DIM // 2, axis=1))
        return xn * rc + partner * rs

    def keys_of(ref, c, j, r):
        w0 = j * ATT_BLOCK + r * SUB
        return jnp.concatenate([ref[c, w0:w0 + WIN, :], ref[c, BOT_ROW + w0:BOT_ROW + w0 + WIN, :],
                                ref[c, META_ROW:META_ROW + ATT_BLOCK, :]], axis=0)

    attn_units, mlstm_units = [], []

    for j in range(NBLK):
        r0 = j * ATT_BLOCK
        blk = i * NBLK + j
        ctx = {}

        def kv_update(j=j, r0=r0, blk=blk, ctx=ctx):
            rc = rc_ref[r0:r0 + ATT_BLOCK, :]
            rs = rs_ref[r0:r0 + ATT_BLOCK, :]
            kx = z_ref[r0:r0 + ATT_BLOCK, OFF_K:OFF_K + LANES]
            kp = rope(kx * head_rms_inv(kx) * kg_ref[...], rc, rs)
            vp = z_ref[r0:r0 + ATT_BLOCK, OFF_V:OFF_V + LANES]
            kp_sw = pltpu.roll(kp, HEAD_DIM, axis=1)
            vp_sw = pltpu.roll(vp, HEAD_DIM, axis=1)
            slot = (j + 1) * ATT_BLOCK
            for c in range(KV_HEADS):
                for ref, own, swapped in ((kk_ref, kp, kp_sw), (vv_ref, vp, vp_sw)):
                    top = jnp.where(lo_half, own if c == 0 else swapped, 0.0).astype(BF16)
                    bot = jnp.where(lo_half, 0.0, swapped if c == 0 else own).astype(BF16)
                    ref[c, slot:slot + ATT_BLOCK, 0:LANES] = top
                    ref[c, BOT_ROW + slot:BOT_ROW + slot + ATT_BLOCK, 0:LANES] = bot
                    if j == META_BLK % NBLK:
                        for m0, part in ((META_ROW, top), (META_ROW + N_META, bot)):
                            keep = ref[c, m0:m0 + N_META, 0:LANES].astype(F32)
                            new = part[META_OFF:ATT_BLOCK, :].astype(F32)
                            ref[c, m0:m0 + N_META, 0:LANES] = jnp.where(blk == META_BLK, new, keep).astype(BF16)

            sl = lax.broadcasted_iota(jnp.int32, (SUB, LANES), 1)
            si = lax.broadcasted_iota(jnp.int32, (SUB, LANES), 0)
            for r in range(2):
                qrow = blk * ATT_BLOCK + r * SUB + si
                oks = []
                for w in (sl, jnp.where(sl < HEAD_DIM, ATT_BLOCK + sl, sl - HEAD_DIM), sl + HEAD_DIM):
                    krow = (blk - 1) * ATT_BLOCK + r * SUB + w
                    oks.append((krow >= LEAD) & (w > si) & (w <= si + ATT_BLOCK))
                ctx["band_ok", r] = oks
                meta_ok = (PAD + (sl % N_META)) <= qrow
                ctx["meta_ok", r] = ((sl < N_META) & meta_ok, (sl >= N_META) & (sl < 2 * N_META) & meta_ok)
                ctx["meta_any", r] = (sl < 2 * N_META) & meta_ok
            ctx["sink_lanes"] = (sl >> 1) == (SINK_LANE >> 1)
            ctx["first_head_lanes"] = (sl < N_META) | (sl == SINK_LANE)
            ctx["lane"] = sl

        def q_scores(c, j=j, r0=r0, ctx=ctx):
            rc = rc_ref[r0:r0 + ATT_BLOCK, :]
            rs = rs_ref[r0:r0 + ATT_BLOCK, :]
            gain2 = jnp.concatenate([qg_ref[...], qg_ref[...]], axis=1)
            q4 = []
            for half in range(PAIRS_PER_KV // 2):
                c0 = OFF_Q + (c * PAIRS_PER_KV + 2 * half) * LANES
                xq = z_ref[r0:r0 + ATT_BLOCK, c0:c0 + 2 * LANES]
                xn = xq * head_rms_inv(xq) * gain2
                q4.append(rope(xn[:, 0:LANES], rc, rs).astype(BF16))
                q4.append(rope(xn[:, LANES:2 * LANES], rc, rs).astype(BF16))
            for r in range(2):
                q4r = jnp.concatenate([t[r * SUB:(r + 1) * SUB, :] for t in q4], axis=0)
                ctx["s", c, r] = _dot_nt(q4r, keys_of(kk_ref, c, j, r))

        def softmax(c, pi, j=j, ctx=ctx):
            p = c * PAIRS_PER_KV + pi
            pbuf = p_ref.at[j * KV_HEADS + c]
            sl = ctx["lane"]
            lo = sl < HEAD_DIM
            sinks = (sink_ref[2 * p], sink_ref[2 * p + 1])
            sink_pair = jnp.where(sl == SINK_LANE, sinks[0], sinks[1])
            for r in range(2):
                sc = ctx["s", c, r][pi * SUB:(pi + 1) * SUB, :]
                rows = slice((r * PAIRS_PER_KV + pi) * SUB, (r * PAIRS_PER_KV + pi + 1) * SUB)
                b0, b1, b2 = (jnp.where(ok, sc[:, t * LANES:(t + 1) * LANES], NEG)
                              for t, ok in enumerate(ctx["band_ok", r]))
                sm = sc[:, META_COL:META_COL + LANES]
                sm_a = jnp.where(ctx["meta_ok", r][0], sm, NEG)
                sm_b = jnp.where(ctx["meta_ok", r][1], sm, NEG)
                both_a = jnp.maximum(jnp.maximum(b0, jnp.where(lo, b1, NEG)), sm_a)
                both_b = jnp.maximum(jnp.maximum(b2, jnp.where(lo, NEG, b1)), sm_b)
                mx_a = jnp.maximum(jnp.max(both_a, axis=-1, keepdims=True), sinks[0])
                mx_b = jnp.maximum(jnp.max(both_b, axis=-1, keepdims=True), sinks[1])
                pbuf[rows, 0:LANES] = jnp.exp2(b0 - mx_a).astype(BF16)
                pbuf[rows, LANES:2 * LANES] = jnp.exp2(b1 - jnp.where(lo, mx_a, mx_b)).astype(BF16)
                pbuf[rows, 2 * LANES:3 * LANES] = jnp.exp2(b2 - mx_b).astype(BF16)
                sm2 = jnp.where(ctx["meta_any", r], sm, jnp.where(ctx["sink_lanes"], sink_pair, NEG))
                mx2 = jnp.where(ctx["first_head_lanes"], mx_a, mx_b)
                pbuf[rows, META_COL:META_COL + LANES] = jnp.exp2(sm2 - mx2).astype(BF16)

        def pv(c, j=j, r0=r0):
            for r in range(2):
                o4 = _dot(p_ref[j * KV_HEADS + c, r * PAIRS_PER_KV * SUB:(r + 1) * PAIRS_PER_KV * SUB, :],
                          keys_of(vv_ref, c, j, r))
                y0 = r0 + r * SUB
                for pi in range(PAIRS_PER_KV):
                    p = c * PAIRS_PER_KV + pi
                    rows = slice(pi * SUB, (pi + 1) * SUB)
                    gate = _silu(z_ref[y0:y0 + SUB, OFF_AG + p * LANES:OFF_AG + (p + 1) * LANES])
                    yw_ref[y0:y0 + SUB, p * LANES:(p + 1) * LANES] = (
                        o4[rows, 0:LANES] * (1.0 / o4[rows, LANES:2 * LANES]) * gate).astype(BF16)

        attn_units.append((kv_update, 1))
        for c in range(KV_HEADS):
            attn_units.append((functools.partial(q_scores, c), 0))
            attn_units.extend((functools.partial(softmax, c, pi), 1) for pi in range(PAIRS_PER_KV))
            attn_units.append((functools.partial(pv, c), 0))

    def kv_carry():
        for ref in (kk_ref, vv_ref):
            for c in range(KV_HEADS):
                for base in (0, BOT_ROW):
                    ref[c, base:base + ATT_BLOCK, 0:LANES] = (
                        ref[c, base + NBLK * ATT_BLOCK:base + (NBLK + 1) * ATT_BLOCK, 0:LANES])

    crow = lax.broadcasted_iota(jnp.int32, (CHUNK, LANES), 0)
    tril = (lax.broadcasted_iota(jnp.int32, (CHUNK, CHUNK), 0)
            >= lax.broadcasted_iota(jnp.int32, (CHUNK, CHUNK), 1))
    for cj in range(nchunk):
        q0 = cj * CHUNK
        g = {}

        def chunk_prep(cj=cj, q0=q0, g=g):
            if cj == 0:
                u_ref[0:SUBLANES, :] = u_ref[tb:tb + SUBLANES, :]
                u_ref[SUBLANES:SUBLANES + tb, :] = z_ref[:, OFF_MQ:OFF_MQ + QK_CONV_COLS]
            real =(i * tb + q0 + crow) >= PAD
            real_col = (i * tb + q0 + lax.broadcasted_iota(jnp.int32, (CHUNK, 1), 0)) >= PAD
            uu = u_ref[q0:q0 + SUBLANES + CHUNK, :]
            acc = cw_ref[0:1, :] * uu
            for t in range(1, CONV_K):
                acc = cw_ref[t:t + 1, :] * uu + pltpu.roll(acc, 1, axis=0)
            conv = acc[SUBLANES:, :] + cb_ref[...]
            g["qk"] = jnp.where(real_col, _silu(conv), 0.0)

            li = jnp.where(real, z_ref[q0:q0 + CHUNK, OFF_GI:OFF_GI + LANES] + bi_ref[...], NEG)
            lf = jnp.where(real, _log_sigmoid(z_ref[q0:q0 + CHUNK, OFF_GF:OFF_GF + LANES] + bf_ref[...]), 0.0)
            b = _row_scan(lf, crow, jnp.add, 0.0)
            r = li - b
            rmax = jnp.max(r, axis=0, keepdims=True)
            g["wg"] = jnp.exp(r - rmax)
            dmax = b + _row_scan(r, crow, jnp.maximum, NEG)
            m_in = m_ref[0:1, :]
            a_t = b + m_in
            m_t = jnp.maximum(a_t, dmax)
            g["inter"] = jnp.exp(a_t - m_t)
            g["cvec"] = b - m_t
            g["em_t"] = jnp.exp(-m_t)
            g["r_t"] = r.T
            b_last = b[CHUNK - 1:CHUNK, :]
            mg = b_last + rmax
            m_new = jnp.maximum(b_last + m_in, mg)
            g["a_st"] = jnp.exp(b_last + m_in - m_new)
            g["w_st"] = jnp.exp(mg - m_new)
            m_ref[...] = jnp.broadcast_to(m_new, (SUBLANES, LANES))

        def heads_state(q0=q0, g=g):
            for hd in range(M_HEADS):
                qk = g["qk"]
                q = qk[:, hd * M_QK_DIM:(hd + 1) * M_QK_DIM]
                k = qk[:, (M_HEADS + hd) * M_QK_DIM:(M_HEADS + hd + 1) * M_QK_DIM] * (M_QK_DIM ** -0.5)
                vb = z_ref[q0:q0 + CHUNK, OFF_MV + hd * M_V_DIM:OFF_MV + (hd + 1) * M_V_DIM].astype(BF16)
                qb = q.astype(BF16)
                c_in = c_ref[hd]
                n_in = n_ref[hd, 0:1, :]
                g["s", hd] = _dot_nt(qb, k.astype(BF16))
                g["qc", hd] = _dot(qb, c_in.astype(BF16))
                g["qn", hd] = jnp.sum(q * n_in, axis=-1, keepdims=True)
                g["vb", hd] = vb

                kw = g["wg"][:, hd:hd + 1] * k
                a_s = g["a_st"][:, hd:hd + 1]
                w_s = g["w_st"][:, hd:hd + 1]
                c_ref[hd] = a_s * c_in + w_s * _dot_tn(kw.astype(BF16), vb)
                n_new = a_s * n_in + w_s * jnp.sum(kw, axis=0, keepdims=True)
                n_ref[hd] = jnp.broadcast_to(n_new, (SUBLANES, LANES))

        def head_out(hd, q0=q0, g=g):
            arg = g["cvec"][:, hd:hd + 1] + g["r_t"][hd:hd + 1, :]
            sg = g["s", hd] * jnp.exp(jnp.where(tril, arg, NEG))
            inter_c = g["inter"][:, hd:hd + 1]
            num = inter_c * g["qc", hd] + _dot(sg.astype(BF16), g["vb", hd])
            den = inter_c * g["qn", hd] + jnp.sum(sg, axis=-1, keepdims=True)
            hout = num / jnp.maximum(jnp.abs(den), g["em_t"][:, hd:hd + 1])

            cols = slice(hd * M_V_DIM, (hd + 1) * M_V_DIM)
            hm = hout * lax.rsqrt(jnp.mean(hout * hout, axis=-1, keepdims=True) + EPS) * og_ref[:, cols]
            ocols = slice(OFF_MO + hd * M_V_DIM, OFF_MO + (hd + 1) * M_V_DIM)
            gcols = slice(OFF_MG + hd * M_V_DIM, OFF_MG + (hd + 1) * M_V_DIM)
            ym = hm * _sigmoid(z_ref[q0:q0 + CHUNK, ocols]) * _silu(z_ref[q0:q0 + CHUNK, gcols])
            yw_ref[q0:q0 + CHUNK, D_MODEL + hd * M_V_DIM:D_MODEL + (hd + 1) * M_V_DIM] = ym.astype(BF16)

        mlstm_units.append((chunk_prep, 1))
        mlstm_units.append((heads_state, 0))
        mlstm_units.extend((functools.partial(head_out, hd), 1) for hd in range(M_HEADS))

    kv_carry()
    if len(attn_units) == len(UNIT_ORDER_A) and len(mlstm_units) == len(UNIT_ORDER_M):
        order = sorted([(pos, u) for pos, u in zip(UNIT_ORDER_A, attn_units)]
                       + [(pos, u) for pos, u in zip(UNIT_ORDER_M, mlstm_units)], key=lambda e: e[0])
        units = [u for _, u in order]
    else:
        units = _interleave(attn_units, mlstm_units)
    n_receivers = sum(takes for _, takes in units)
    n_spread = len(jobs) - TAIL_JOBS
    seen = 0
    for unit, takes in units:
        seen += takes
        while takes and len(jobs) > TAIL_JOBS and (n_spread + TAIL_JOBS - len(jobs)) * n_receivers < seen * n_spread:
            jobs.pop(0)()
        unit()
    for job in jobs:
        job()


def _layer_call(h, head, sink, rc, rs, ng, win, qg, kg, bd, cw, cb, bi, bf, og, wout, *, layer, first, last):
    batch, rows, d = h.shape
    tb = ROW_TILE
    nt = (rows + LEAD if first else rows) // tb
    last_tile = batch * nt - 1

    def tile_of(t):
        t = jnp.clip(t, 0, last_tile)
        return t // nt, t % nt, 0

    def token_tile_of(t):
        b, i, _ = tile_of(t)
        return b, jnp.maximum(i - LEAD // tb, 0), 0

    const = lambda s: (0, 0)
    in_tile = token_tile_of if first else tile_of
    mix_rows = lambda s: tile_of(s - 1)[1:]
    out_tile = token_tile_of if last else tile_of
    out_rows = rows + (LEAD if first else 0) - (LEAD if last else 0)
    once = pl.Buffered(1)
    full = lambda a: pl.BlockSpec(a.shape, const)
    in_specs = [
        pl.BlockSpec(memory_space=pltpu.SMEM),
        pl.BlockSpec((1, tb, d), lambda s: in_tile(s)),
        pl.BlockSpec((1, tb, d), lambda s: in_tile(s - 2)),
        full(head),
        pl.BlockSpec((tb, LANES), mix_rows),
        pl.BlockSpec((tb, LANES), mix_rows),
        full(ng),
        *[pl.BlockSpec((None,) + w.shape[1:], lambda s: (layer, 0, 0), pipeline_mode=once) for w in win],
        full(qg), full(kg), full(bd), full(cw), full(cb), full(bi), full(bf), full(og),
        pl.BlockSpec((None,) + wout.shape[1:], lambda s: (layer, 0, 0), pipeline_mode=once),
    ]
    scratch = [
        pltpu.VMEM((tb, IN_COLS), F32),
        pltpu.VMEM((tb, IN_COLS), F32),
        pltpu.VMEM((tb, MIX_COLS), BF16),
        pltpu.VMEM((tb, MIX_COLS), BF16),
        pltpu.VMEM((tb, D_MODEL), BF16),
        pltpu.VMEM((KV_HEADS, KEY_ROWS, LANES), BF16),
        pltpu.VMEM((KV_HEADS, KEY_ROWS, 2 * LANES), BF16),
        pltpu.VMEM((NBLK * KV_HEADS, PAIRS_PER_KV * ATT_BLOCK, KEY_COLS), BF16),
        pltpu.VMEM((tb + SUBLANES, QK_CONV_COLS), F32),
        pltpu.VMEM((M_HEADS, M_QK_DIM, M_V_DIM), F32),
        pltpu.VMEM((M_HEADS, SUBLANES, LANES), F32),
        pltpu.VMEM((SUBLANES, LANES), F32),
    ]
    return pl.pallas_call(
        functools.partial(_layer_kernel, nt=nt, n_tiles=batch * nt, first=first),
        out_shape=jax.ShapeDtypeStruct((batch, out_rows, d), h.dtype),
        grid=(batch * nt + 2,),
        in_specs=in_specs,
        out_specs=pl.BlockSpec((1, tb, d), lambda s: out_tile(s - 2)),
        scratch_shapes=scratch,
        compiler_params=pltpu.CompilerParams(
            dimension_semantics=("arbitrary",),
            vmem_limit_bytes=VMEM_LIMIT_BYTES),
        name="hybrid_layer",
    )(sink, h, h, head, rc, rs, ng, *win, qg, kg, bd, cw, cb, bi, bf, og, wout)


def _rope_tables(lp):
    pos = jnp.arange(lp, dtype=F32) - PAD
    inv_freq = ROPE_THETA ** (-jnp.arange(0, ROT_DIM, 2, dtype=F32) / ROT_DIM)
    ang = pos[:, None] * inv_freq[None, :]
    cos, sin = jnp.cos(ang), jnp.sin(ang)
    ones = jnp.ones((lp, HEAD_DIM - ROT_DIM), F32)
    zeros = jnp.zeros((lp, HEAD_DIM - ROT_DIM), F32)
    rc = jnp.concatenate([cos, cos, ones], axis=1)
    rs = jnp.concatenate([-sin, sin, zeros], axis=1)
    return jnp.tile(rc, (1, 2)), jnp.tile(rs, (1, 2))


def _pad_lanes(v):
    return jnp.pad(v.astype(F32), (0, LANES - v.shape[0]))[None, :]


def kernel(x, meta, norm_g, w_in, attn_q_norm_g, attn_k_norm_g, attn_sink, mlstm_conv_w,
           mlstm_conv_b, mlstm_b_i, mlstm_b_f, mlstm_out_norm_g, w_out):
    batch, seq, d = x.shape
    depth = w_in.shape[0]
    assert d == D_MODEL and seq % ROW_TILE == 0
    lp = seq + LEAD
    head = jnp.concatenate([jnp.zeros((PAD, d), x.dtype), meta.astype(x.dtype)], axis=0)
    h = x
    rc, rs = _rope_tables(lp)
    head_id = jnp.arange(2 * LANES) // HEAD_DIM
    ones_bd = (head_id[:, None] == head_id[None, :]).astype(BF16)

    gate_lo = OFF_MO
    n_gate = 2 * M_HEADS
    lane_pad = ((0, 0), (0, 0), (0, LANES - M_HEADS))
    gi = jnp.pad(w_in[:, :, gate_lo:gate_lo + M_HEADS], lane_pad)
    gf = jnp.pad(w_in[:, :, gate_lo + M_HEADS:gate_lo + n_gate], lane_pad)
    stride_pad = ((0, 0), (0, 0), (0, WOUT_PAD_COLS))
    win = (w_in[:, :, :gate_lo].astype(BF16),
           jnp.pad(w_in[:, :, gate_lo + n_gate:].astype(BF16), stride_pad),
           jnp.pad(jnp.concatenate([gi, gf], axis=2).astype(BF16), stride_pad))
    wout = jnp.pad(w_out.astype(BF16), ((0, 0), (0, 0), (0, WOUT_PAD_COLS)))
    for l in range(depth):
        h = _layer_call(
            h, head, attn_sink[l].astype(F32) * LOG2_E, rc, rs,
            norm_g[l].astype(F32)[None, :], win,
            (jnp.tile(attn_q_norm_g[l].astype(F32), 2) * (HEAD_DIM ** -0.5 * LOG2_E))[None, :],
            jnp.tile(attn_k_norm_g[l].astype(F32), 2)[None, :], ones_bd,
            mlstm_conv_w[l].astype(F32), mlstm_conv_b[l].astype(F32)[None, :],
            _pad_lanes(mlstm_b_i[l]), _pad_lanes(mlstm_b_f[l]),
            mlstm_out_norm_g[l].astype(F32)[None, :], wout,
            layer=l, first=(l == 0), last=(l == depth - 1))
    return h
```

```python
import functools

import jax
import jax.numpy as jnp
from jax import lax
from jax.experimental import pallas as pl
from jax.experimental.pallas import tpu as pltpu

D_MODEL = 1024
N_META = 16
HEAD_DIM = 64
ATT_HEADS = 16
KV_HEADS = 2
ATT_BLOCK = 128
ROT_DIM = 16
ROPE_THETA = 500000.0
M_HEADS = 4
M_V_DIM = 256
M_QK_DIM = 128
CHUNK = 128
CONV_K = 4
EPS = 1e-6
NEG = -1e30
LOG2_E = 1.4426950408889634
NEG_LOG2_E = -LOG2_E

LANES = 128
SUBLANES = 8
MXU_COLS = 256
HEAD_PAIRS = ATT_HEADS // 2
PAIRS_PER_KV = HEAD_PAIRS // KV_HEADS
SUB = ATT_BLOCK // 2
WIN = SUB + ATT_BLOCK
KEY_COLS = 2 * WIN + LANES
META_COL = 2 * WIN
SINK_LANE = 2 * N_META

ROW_TILE = 128
LEAD = ROW_TILE
PAD = LEAD - N_META
META_BLK = PAD // ATT_BLOCK
META_OFF = PAD % ATT_BLOCK
NBLK = ROW_TILE // ATT_BLOCK
BOT_ROW = (NBLK + 1) * ATT_BLOCK
META_ROW = 2 * BOT_ROW
KEY_ROWS = META_ROW + ATT_BLOCK

OFF_Q = 0
OFF_K = 1024
OFF_V = 1152
OFF_AG = 1280
OFF_MQ = 2304
OFF_MK = 2816
OFF_MV = 3328
OFF_MO = 4352
OFF_MG = 5376
OFF_GI = 6400
OFF_GF = 6528
IN_COLS = 6656
QK_CONV_COLS = 2 * M_HEADS * M_QK_DIM
MIX_COLS = 2 * D_MODEL

IN_JOB_COLS = MXU_COLS
OUT_JOB_COLS = MXU_COLS
TAIL_JOBS = 0
UNIT_ORDER_A = (0, 1, 3, 5, 7, 9, 12, 10, 14, 16, 18, 20, 22)
UNIT_ORDER_M = (2, 4, 6, 8, 11, 13, 15, 17, 19, 21, 23, 24)
WOUT_PAD_COLS = LANES
VMEM_LIMIT_BYTES = 56 * 1024 * 1024

F32 = jnp.float32
BF16 = jnp.bfloat16


def _sigmoid(x):
    return 1.0 / (1.0 + jnp.exp2(x * NEG_LOG2_E))


def _silu(x):
    return x * _sigmoid(x)


def _log_sigmoid(x):
    return jnp.minimum(x, 0.0) - jnp.log1p(jnp.exp(-jnp.abs(x)))


def _dot(a, b):
    return jnp.dot(a, b, preferred_element_type=F32)


def _dot_nt(a, b):
    return lax.dot_general(a, b, (((1,), (1,)), ((), ())), preferred_element_type=F32)


def _dot_tn(a, b):
    return lax.dot_general(a, b, (((0,), (0,)), ((), ())), preferred_element_type=F32)


def _row_scan(x, row, combine, fill):
    shift = 1
    while shift < x.shape[0]:
        x = combine(x, jnp.where(row >= shift, pltpu.roll(x, shift, axis=0), fill))
        shift *= 2
    return x


def _interleave(a, b):
    out, ia, ib = [], 0, 0
    while ia < len(a) or ib < len(b):
        if ib >= len(b) or (ia < len(a) and ia * len(b) <= ib * len(a)):
            out.append(a[ia]); ia += 1
        else:
            out.append(b[ib]); ib += 1
    return out


def _layer_kernel(sink_ref, hn_ref, h_ref, head_ref, rc_ref, rs_ref, ng_ref, wa_ref, wb_ref, wg_ref,
                  qg_ref, kg_ref, bd_ref,
                  cw_ref, cb_ref, bi_ref, bf_ref, og_ref, wout_ref, o_ref,
                  za_ref, zb_ref, ya_ref, yb_ref, hb_ref, kk_ref, vv_ref, p_ref, u_ref,
                  c_ref, n_ref, m_ref, *, nt, n_tiles, first):
    s = pl.program_id(0)

    @pl.when(s == 0)
    def _():
        zb_ref[...] = jnp.zeros_like(zb_ref)
        yb_ref[...] = jnp.zeros_like(yb_ref)

    @pl.when((s == 0) | (lax.rem(s - 1, nt) == 0))
    def _():
        kk_ref[...] = jnp.zeros_like(kk_ref)
        r = lax.broadcasted_iota(jnp.int32, (KEY_ROWS, LANES), 0)
        l = lax.broadcasted_iota(jnp.int32, (KEY_ROWS, LANES), 1)
        m = r - META_ROW
        top = (r < BOT_ROW) | ((m >= 0) & (m < N_META)) | (m == SINK_LANE)
        bot = ((r >= BOT_ROW) & (r < META_ROW)) | ((m >= N_META) & (m < SINK_LANE)) | (m == SINK_LANE + 1)
        ones = jnp.where(l < HEAD_DIM, jnp.where(top, 1.0, 0.0), jnp.where(bot, 1.0, 0.0)).astype(BF16)
        for c in range(KV_HEADS):
            vv_ref[c, :, 0:LANES] = jnp.zeros((KEY_ROWS, LANES), BF16)
            vv_ref[c, :, LANES:2 * LANES] = ones
        u_ref[ROW_TILE:ROW_TILE + SUBLANES, :] = jnp.zeros((SUBLANES, QK_CONV_COLS), F32)
        c_ref[...] = jnp.zeros_like(c_ref)
        n_ref[...] = jnp.zeros_like(n_ref)
        m_ref[...] = jnp.zeros_like(m_ref)

    step = functools.partial(
        _layer_step, sink_ref, hn_ref, h_ref, head_ref, rc_ref, rs_ref, ng_ref, (wa_ref, wb_ref, wg_ref),
        qg_ref, kg_ref, bd_ref,
        cw_ref, cb_ref, bi_ref, bf_ref, og_ref, wout_ref, o_ref,
        hb_ref, kk_ref, vv_ref, p_ref, u_ref, c_ref, n_ref, m_ref, nt=nt, n_tiles=n_tiles, first=first)

    @pl.when(s % 2 == 0)
    def _():
        step(za_ref, zb_ref, ya_ref, yb_ref)

    @pl.when(s % 2 == 1)
    def _():
        step(zb_ref, za_ref, yb_ref, ya_ref)


def _layer_step(sink_ref, hn_ref, h_ref, head_ref, rc_ref, rs_ref, ng_ref, win_refs, qg_ref, kg_ref, bd_ref,
                cw_ref, cb_ref, bi_ref, bf_ref, og_ref, wout_ref, o_ref,
                hb_ref, kk_ref, vv_ref, p_ref, u_ref, c_ref, n_ref, m_ref,
                zw_ref, z_ref, yw_ref, yr_ref, *, nt, n_tiles, first):
    tb = ROW_TILE
    s = pl.program_id(0)
    i = jnp.where(s < 1, -1, lax.rem(s - 1, nt))
    i_out = jnp.where(s < 2, -1, lax.rem(s - 2, nt))
    nchunk = tb // CHUNK

    hx = hn_ref[0]
    if first:
        hx = jnp.where(lax.rem(jnp.minimum(s, n_tiles - 1), nt) == 0, head_ref[...], hx)
    hb_ref[...] = (hx * lax.rsqrt(jnp.mean(hx * hx, axis=-1, keepdims=True) + EPS)
                   * ng_ref[...]).astype(BF16)
    orow = i_out * tb + lax.broadcasted_iota(jnp.int32, (tb, 1), 0)

    def in_proj_job(c0):
        ref, base = [(r, b) for r, b in zip(win_refs, (0, OFF_MO, OFF_GI)) if b <= c0][-1]

        def run():
            zw_ref[:, c0:c0 + IN_JOB_COLS] = _dot(hb_ref[...], ref[:, c0 - base:c0 - base + IN_JOB_COLS])
        return run

    def out_proj_job(c0):
        def run():
            y = (_dot(yr_ref[:, 0:D_MODEL], wout_ref[0:D_MODEL, c0:c0 + OUT_JOB_COLS])
                 + _dot(yr_ref[:, D_MODEL:MIX_COLS], wout_ref[D_MODEL:MIX_COLS, c0:c0 + OUT_JOB_COLS]))
            res = h_ref[0, :, c0:c0 + OUT_JOB_COLS]
            if first:
                res = jnp.where(i_out == 0, head_ref[:, c0:c0 + OUT_JOB_COLS], res)
            o_ref[0, :, c0:c0 + OUT_JOB_COLS] = jnp.where(orow >= PAD, res + y, 0.0)
        return run

    jobs = _interleave([in_proj_job(c0) for c0 in range(0, IN_COLS, IN_JOB_COLS)],
                       [out_proj_job(c0) for c0 in range(0, D_MODEL, OUT_JOB_COLS)])

    lane = lax.broadcasted_iota(jnp.int32, (ATT_BLOCK, LANES), 1)
    lo_half = lane < HEAD_DIM
    rot_lo = (lane % HEAD_DIM) < (ROT_DIM // 2)

    def head_rms_inv(x):
        w = x.shape[1]
        ss = _dot((x * x).astype(BF16), bd_ref[0:w, 0:w])
        return lax.rsqrt(ss * (1.0 / HEAD_DIM) + EPS)

    def rope(xn, rc, rs):
        partner = jnp.where(rot_lo, pltpu.roll(xn, LANES - ROT_DIM // 2, axis=1),
                            pltpu.roll(xn, ROT_DIM // 2, axis=1))
        return xn * rc + partner * rs

    def keys_of(ref, c, j, r):
        w0 = j * ATT_BLOCK + r * SUB
        return jnp.concatenate([ref[c, w0:w0 + WIN, :], ref[c, BOT_ROW + w0:BOT_ROW + w0 + WIN, :],
                                ref[c, META_ROW:META_ROW + ATT_BLOCK, :]], axis=0)

    attn_units, mlstm_units = [], []

    for j in range(NBLK):
        r0 = j * ATT_BLOCK
        blk = i * NBLK + j
        ctx = {}

        def kv_update(j=j, r0=r0, blk=blk, ctx=ctx):
            rc = rc_ref[r0:r0 + ATT_BLOCK, :]
            rs = rs_ref[r0:r0 + ATT_BLOCK, :]
            kx = z_ref[r0:r0 + ATT_BLOCK, OFF_K:OFF_K + LANES]
            kp = rope(kx * head_rms_inv(kx) * kg_ref[...], rc, rs)
            vp = z_ref[r0:r0 + ATT_BLOCK, OFF_V:OFF_V + LANES]
            kp_sw = pltpu.roll(kp, HEAD_DIM, axis=1)
            vp_sw = pltpu.roll(vp, HEAD_DIM, axis=1)
            slot = (j + 1) * ATT_BLOCK
            for c in range(KV_HEADS):
                for ref, own, swapped in ((kk_ref, kp, kp_sw), (vv_ref, vp, vp_sw)):
                    top = jnp.where(lo_half, own if c == 0 else swapped, 0.0).astype(BF16)
                    bot = jnp.where(lo_half, 0.0, swapped if c == 0 else own).astype(BF16)
                    ref[c, slot:slot + ATT_BLOCK, 0:LANES] = top
                    ref[c, BOT_ROW + slot:BOT_ROW + slot + ATT_BLOCK, 0:LANES] = bot
                    if j == META_BLK % NBLK:
                        for m0, part in ((META_ROW, top), (META_ROW + N_META, bot)):
                            keep = ref[c, m0:m0 + N_META, 0:LANES].astype(F32)
                            new = part[META_OFF:ATT_BLOCK, :].astype(F32)
                            ref[c, m0:m0 + N_META, 0:LANES] = jnp.where(blk == META_BLK, new, keep).astype(BF16)

            sl = lax.broadcasted_iota(jnp.int32, (SUB, LANES), 1)
            si = lax.broadcasted_iota(jnp.int32, (SUB, LANES), 0)
            for r in range(2):
                qrow = blk * ATT_BLOCK + r * SUB + si
                oks = []
                for w in (sl, jnp.where(sl < HEAD_DIM, ATT_BLOCK + sl, sl - HEAD_DIM), sl + HEAD_DIM):
                    krow = (blk - 1) * ATT_BLOCK + r * SUB + w
                    oks.append((krow >= LEAD) & (w > si) & (w <= si + ATT_BLOCK))
                ctx["band_ok", r] = oks
                meta_ok = (PAD + (sl % N_META)) <= qrow
                ctx["meta_ok", r] = ((sl < N_META) & meta_ok, (sl >= N_META) & (sl < 2 * N_META) & meta_ok)
                ctx["meta_any", r] = (sl < 2 * N_META) & meta_ok
            ctx["sink_lanes"] = (sl >> 1) == (SINK_LANE >> 1)
            ctx["first_head_lanes"] = (sl < N_META) | (sl == SINK_LANE)
            ctx["lane"] = sl

        def q_scores(c, j=j, r0=r0, ctx=ctx):
            rc = rc_ref[r0:r0 + ATT_BLOCK, :]
            rs = rs_ref[r0:r0 + ATT_BLOCK, :]
            gain2 = jnp.concatenate([qg_ref[...], qg_ref[...]], axis=1)
            q4 = []
            for half in range(PAIRS_PER_KV // 2):
                c0 = OFF_Q + (c * PAIRS_PER_KV + 2 * half) * LANES
                xq = z_ref[r0:r0 + ATT_BLOCK, c0:c0 + 2 * LANES]
                xn = xq * head_rms_inv(xq) * gain2
                q4.append(rope(xn[:, 0:LANES], rc, rs).astype(BF16))
                q4.append(rope(xn[:, LANES:2 * LANES], rc, rs).astype(BF16))
            for r in range(2):
                q4r = jnp.concatenate([t[r * SUB:(r + 1) * SUB, :] for t in q4], axis=0)
                ctx["s", c, r] = _dot_nt(q4r, keys_of(kk_ref, c, j, r))

        def softmax(c, pi, j=j, ctx=ctx):
            p = c * PAIRS_PER_KV + pi
            pbuf = p_ref.at[j * KV_HEADS + c]
            sl = ctx["lane"]
            lo = sl < HEAD_DIM
            sinks = (sink_ref[2 * p], sink_ref[2 * p + 1])
            sink_pair = jnp.where(sl == SINK_LANE, sinks[0], sinks[1])
            for r in range(2):
                sc = ctx["s", c, r][pi * SUB:(pi + 1) * SUB, :]
                rows = slice((r * PAIRS_PER_KV + pi) * SUB, (r * PAIRS_PER_KV + pi + 1) * SUB)
                b0, b1, b2 = (jnp.where(ok, sc[:, t * LANES:(t + 1) * LANES], NEG)
                              for t, ok in enumerate(ctx["band_ok", r]))
                sm = sc[:, META_COL:META_COL + LANES]
                sm_a = jnp.where(ctx["meta_ok", r][0], sm, NEG)
                sm_b = jnp.where(ctx["meta_ok", r][1], sm, NEG)
                both_a = jnp.maximum(jnp.maximum(b0, jnp.where(lo, b1, NEG)), sm_a)
                both_b = jnp.maximum(jnp.maximum(b2, jnp.where(lo, NEG, b1)), sm_b)
                mx_a = jnp.maximum(jnp.max(both_a, axis=-1, keepdims=True), sinks[0])
                mx_b = jnp.maximum(jnp.max(both_b, axis=-1, keepdims=True), sinks[1])
                pbuf[rows, 0:LANES] = jnp.exp2(b0 - mx_a).astype(BF16)
                pbuf[rows, LANES:2 * LANES] = jnp.exp2(b1 - jnp.where(lo, mx_a, mx_b)).astype(BF16)
                pbuf[rows, 2 * LANES:3 * LANES] = jnp.exp2(b2 - mx_b).astype(BF16)
                sm2 = jnp.where(ctx["meta_any", r], sm, jnp.where(ctx["sink_lanes"], sink_pair, NEG))
                mx2 = jnp.where(ctx["first_head_lanes"], mx_a, mx_b)
                pbuf[rows, META_COL:META_COL + LANES] = jnp.exp2(sm2 - mx2).astype(BF16)

        def pv(c, j=j, r0=r0):
            for r in range(2):
                o4 = _dot(p_ref[j * KV_HEADS + c, r * PAIRS_PER_KV * SUB:(r + 1) * PAIRS_PER_KV * SUB, :],
                          keys_of(vv_ref, c, j, r))
                y0 = r0 + r * SUB
                for pi in range(PAIRS_PER_KV):
                    p = c * PAIRS_PER_KV + pi
                    rows = slice(pi * SUB, (pi + 1) * SUB)
                    gate = _silu(z_ref[y0:y0 + SUB, OFF_AG + p * LANES:OFF_AG + (p + 1) * LANES])
                    yw_ref[y0:y0 + SUB, p * LANES:(p + 1) * LANES] = (
                        o4[rows, 0:LANES] * (1.0 / o4[rows, LANES:2 * LANES]) * gate).astype(BF16)

        attn_units.append((kv_update, 1))
        for c in range(KV_HEADS):
            attn_units.append((functools.partial(q_scores, c), 0))
            attn_units.extend((functools.partial(softmax, c, pi), 1) for pi in range(PAIRS_PER_KV))
            attn_units.append((functools.partial(pv, c), 0))

    def kv_carry():
        for ref in (kk_ref, vv_ref):
            for c in range(KV_HEADS):
                for base in (0, BOT_ROW):
                    ref[c, base:base + ATT_BLOCK, 0:LANES] = (
                        ref[c, base + NBLK * ATT_BLOCK:base + (NBLK + 1) * ATT_BLOCK, 0:LANES])

    crow = lax.broadcasted_iota(jnp.int32, (CHUNK, LANES), 0)
    tril = (lax.broadcasted_iota(jnp.int32, (CHUNK, CHUNK), 0)
            >= lax.broadcasted_iota(jnp.int32, (CHUNK, CHUNK), 1))
    for cj in range(nchunk):
        q0 = cj * CHUNK
        g = {}

        def chunk_prep(cj=cj, q0=q0, g=g):
            if cj == 0:
                u_ref[0:SUBLANES, :] = u_ref[tb:tb + SUBLANES, :]
                u_ref[SUBLANES:SUBLANES + tb, :] = z_ref[:, OFF_MQ:OFF_MQ + QK_CONV_COLS]
            real = (i * tb + q0 + crow) >= PAD
            real_col = (i * tb + q0 + lax.broadcasted_iota(jnp.int32, (CHUNK, 1), 0)) >= PAD
            uu = u_ref[q0:q0 + SUBLANES + CHUNK, :]
            acc = cw_ref[0:1, :] * uu
            for t in range(1, CONV_K):
                acc = cw_ref[t:t + 1, :] * uu + pltpu.roll(acc, 1, axis=0)
            conv = acc[SUBLANES:, :] + cb_ref[...]
            g["qk"] = jnp.where(real_col, _silu(conv), 0.0)

            li = jnp.where(real, z_ref[q0:q0 + CHUNK, OFF_GI:OFF_GI + LANES] + bi_ref[...], NEG)
            lf = jnp.where(real, _log_sigmoid(z_ref[q0:q0 + CHUNK, OFF_GF:OFF_GF + LANES] + bf_ref[...]), 0.0)
            b = _row_scan(lf, crow, jnp.add, 0.0)
            r = li - b
            rmax = jnp.max(r, axis=0, keepdims=True)
            g["wg"] = jnp.exp(r - rmax)
            dmax = b + _row_scan(r, crow, jnp.maximum, NEG)
            m_in = m_ref[0:1, :]
            a_t = b + m_in
            m_t = jnp.maximum(a_t, dmax)
            g["inter"] = jnp.exp(a_t - m_t)
            g["cvec"] = b - m_t
            g["em_t"] = jnp.exp(-m_t)
            g["r_t"] = r.T
            b_last = b[CHUNK - 1:CHUNK, :]
            mg = b_last + rmax
            m_new = jnp.maximum(b_last + m_in, mg)
            g["a_st"] = jnp.exp(b_last + m_in - m_new)
            g["w_st"] = jnp.exp(mg - m_new)
            m_ref[...] = jnp.broadcast_to(m_new, (SUBLANES, LANES))

        def heads_state(q0=q0, g=g):
            for hd in range(M_HEADS):
                qk = g["qk"]
                q = qk[:, hd * M_QK_DIM:(hd + 1) * M_QK_DIM]
                k = qk[:, (M_HEADS + hd) * M_QK_DIM:(M_HEADS + hd + 1) * M_QK_DIM] * (M_QK_DIM ** -0.5)
                vb = z_ref[q0:q0 + CHUNK, OFF_MV + hd * M_V_DIM:OFF_MV + (hd + 1) * M_V_DIM].astype(BF16)
                qb = q.astype(BF16)
                c_in = c_ref[hd]
                n_in = n_ref[hd, 0:1, :]
                g["s", hd] = _dot_nt(qb, k.astype(BF16))
                g["qc", hd] = _dot(qb, c_in.astype(BF16))
                g["qn", hd] = jnp.sum(q * n_in, axis=-1, keepdims=True)
                g["vb", hd] = vb

                kw = g["wg"][:, hd:hd + 1] * k
                a_s = g["a_st"][:, hd:hd + 1]
                w_s = g["w_st"][:, hd:hd + 1]
                c_ref[hd] = a_s * c_in + w_s * _dot_tn(kw.astype(BF16), vb)
                n_new = a_s * n_in + w_s * jnp.sum(kw, axis=0, keepdims=True)
                n_ref[hd] = jnp.broadcast_to(n_new, (SUBLANES, LANES))

        def head_out(hd, q0=q0, g=g):
            arg = g["cvec"][:, hd:hd + 1] + g["r_t"][hd:hd + 1, :]
            sg = g["s", hd] * jnp.exp(jnp.where(tril, arg, NEG))
            inter_c = g["inter"][:, hd:hd + 1]
            num = inter_c * g["qc", hd] + _dot(sg.astype(BF16), g["vb", hd])
            den = inter_c * g["qn", hd] + jnp.sum(sg, axis=-1, keepdims=True)
            hout = num / jnp.maximum(jnp.abs(den), g["em_t"][:, hd:hd + 1])

            cols = slice(hd * M_V_DIM, (hd + 1) * M_V_DIM)
            hm = hout * lax.rsqrt(jnp.mean(hout * hout, axis=-1, keepdims=True) + EPS) * og_ref[:, cols]
            ocols = slice(OFF_MO + hd * M_V_DIM, OFF_MO + (hd + 1) * M_V_DIM)
            gcols = slice(OFF_MG + hd * M_V_DIM, OFF_MG + (hd + 1) * M_V_DIM)
            ym = hm * _sigmoid(z_ref[q0:q0 + CHUNK, ocols]) * _silu(z_ref[q0:q0 + CHUNK, gcols])
            yw_ref[q0:q0 + CHUNK, D_MODEL + hd * M_V_DIM:D_MODEL + (hd + 1) * M_V_DIM] = ym.astype(BF16)

        mlstm_units.append((chunk_prep, 1))
        mlstm_units.append((heads_state, 0))
        mlstm_units.extend((functools.partial(head_out, hd), 1) for hd in range(M_HEADS))

    kv_carry()
    if len(attn_units) == len(UNIT_ORDER_A) and len(mlstm_units) == len(UNIT_ORDER_M):
        order = sorted([(pos, u) for pos, u in zip(UNIT_ORDER_A, attn_units)]
                       + [(pos, u) for pos, u in zip(UNIT_ORDER_M, mlstm_units)], key=lambda e: e[0])
        units = [u for _, u in order]
    else:
        units = _interleave(attn_units, mlstm_units)
    n_receivers = sum(takes for _, takes in units)
    n_spread = len(jobs) - TAIL_JOBS
    seen = 0
    for unit, takes in units:
        seen += takes
        while takes and len(jobs) > TAIL_JOBS and (n_spread + TAIL_JOBS - len(jobs)) * n_receivers < seen * n_spread:
            jobs.pop(0)()
        unit()
    for job in jobs:
        job()


def _layer_call(h, head, sink, rc, rs, ng, win, qg, kg, bd, cw, cb, bi, bf, og, wout, *, layer, first, last):
    batch, rows, d = h.shape
    tb = ROW_TILE
    nt = (rows + LEAD if first else rows) // tb
    last_tile = batch * nt - 1

    def tile_of(t):
        t = jnp.clip(t, 0, last_tile)
        return t // nt, t % nt, 0

    def token_tile_of(t):
        b, i, _ = tile_of(t)
        return b, jnp.maximum(i - LEAD // tb, 0), 0

    const = lambda s: (0, 0)
    in_tile = token_tile_of if first else tile_of
    mix_rows = lambda s: tile_of(s - 1)[1:]
    out_tile = token_tile_of if last else tile_of
    out_rows = rows + (LEAD if first else 0) - (LEAD if last else 0)
    once = pl.Buffered(1)
    full = lambda a: pl.BlockSpec(a.shape, const)
    in_specs = [
        pl.BlockSpec(memory_space=pltpu.SMEM),
        pl.BlockSpec((1, tb, d), lambda s: in_tile(s)),
        pl.BlockSpec((1, tb, d), lambda s: in_tile(s - 2)),
        full(head),
        pl.BlockSpec((tb, LANES), mix_rows),
        pl.BlockSpec((tb, LANES), mix_rows),
        full(ng),
        *[pl.BlockSpec((None,) + w.shape[1:], lambda s: (layer, 0, 0), pipeline_mode=once) for w in win],
        full(qg), full(kg), full(bd), full(cw), full(cb), full(bi), full(bf), full(og),
        pl.BlockSpec((None,) + wout.shape[1:], lambda s: (layer, 0, 0), pipeline_mode=once),
    ]
    scratch = [
        pltpu.VMEM((tb, IN_COLS), F32),
        pltpu.VMEM((tb, IN_COLS), F32),
        pltpu.VMEM((tb, MIX_COLS), BF16),
        pltpu.VMEM((tb, MIX_COLS), BF16),
        pltpu.VMEM((tb, D_MODEL), BF16),
        pltpu.VMEM((KV_HEADS, KEY_ROWS, LANES), BF16),
        pltpu.VMEM((KV_HEADS, KEY_ROWS, 2 * LANES), BF16),
        pltpu.VMEM((NBLK * KV_HEADS, PAIRS_PER_KV * ATT_BLOCK, KEY_COLS), BF16),
        pltpu.VMEM((tb + SUBLANES, QK_CONV_COLS), F32),
        pltpu.VMEM((M_HEADS, M_QK_DIM, M_V_DIM), F32),
        pltpu.VMEM((M_HEADS, SUBLANES, LANES), F32),
        pltpu.VMEM((SUBLANES, LANES), F32),
    ]
    return pl.pallas_call(
        functools.partial(_layer_kernel, nt=nt, n_tiles=batch * nt, first=first),
        out_shape=jax.ShapeDtypeStruct((batch, out_rows, d), h.dtype),
        grid=(batch * nt + 2,),
        in_specs=in_specs,
        out_specs=pl.BlockSpec((1, tb, d), lambda s: out_tile(s - 2)),
        scratch_shapes=scratch,
        compiler_params=pltpu.CompilerParams(
            dimension_semantics=("arbitrary",),
            vmem_limit_bytes=VMEM_LIMIT_BYTES),
        name="hybrid_layer",
    )(sink, h, h, head, rc, rs, ng, *win, qg, kg, bd, cw, cb, bi, bf, og, wout)


def _rope_tables(lp):
    pos = jnp.arange(lp, dtype=F32) - PAD
    inv_freq = ROPE_THETA ** (-jnp.arange(0, ROT_DIM, 2, dtype=F32) / ROT_DIM)
    ang = pos[:, None] * inv_freq[None, :]
    cos, sin = jnp.cos(ang), jnp.sin(ang)
    ones = jnp.ones((lp, HEAD_DIM - ROT_DIM), F32)
    zeros = jnp.zeros((lp, HEAD_DIM - ROT_DIM), F32)
    rc = jnp.concatenate([cos, cos, ones], axis=1)
    rs = jnp.concatenate([-sin, sin, zeros], axis=1)
    return jnp.tile(rc, (1, 2)), jnp.tile(rs, (1, 2))


def _pad_lanes(v):
    return jnp.pad(v.astype(F32), (0, LANES - v.shape[0]))[None, :]


def kernel(x, meta, norm_g, w_in, attn_q_norm_g, attn_k_norm_g, attn_sink, mlstm_conv_w,
           mlstm_conv_b, mlstm_b_i, mlstm_b_f, mlstm_out_norm_g, w_out):
    batch, seq, d = x.shape
    depth = w_in.shape[0]
    assert d == D_MODEL and seq % ROW_TILE == 0
    lp = seq + LEAD
    head = jnp.concatenate([jnp.zeros((PAD, d), x.dtype), meta.astype(x.dtype)], axis=0)
    h = x
    rc, rs = _rope_tables(lp)
    head_id = jnp.arange(2 * LANES) // HEAD_DIM
    ones_bd = (head_id[:, None] == head_id[None, :]).astype(BF16)

    gate_lo = OFF_MO
    n_gate = 2 * M_HEADS
    lane_pad = ((0, 0), (0, 0), (0, LANES - M_HEADS))
    gi = jnp.pad(w_in[:, :, gate_lo:gate_lo + M_HEADS], lane_pad)
    gf = jnp.pad(w_in[:, :, gate_lo + M_HEADS:gate_lo + n_gate], lane_pad)
    stride_pad = ((0, 0), (0, 0), (0, WOUT_PAD_COLS))
    win = (w_in[:, :, :gate_lo].astype(BF16),
           jnp.pad(w_in[:, :, gate_lo + n_gate:].astype(BF16), stride_pad),
           jnp.pad(jnp.concatenate([gi, gf], axis=2).astype(BF16), stride_pad))
    wout = jnp.pad(w_out.astype(BF16), ((0, 0), (0, 0), (0, WOUT_PAD_COLS)))
    for l in range(depth):
        h = _layer_call(
            h, head, attn_sink[l].astype(F32) * LOG2_E, rc, rs,
            norm_g[l].astype(F32)[None, :], win,
            (jnp.tile(attn_q_norm_g[l].astype(F32), 2) * (HEAD_DIM ** -0.5 * LOG2_E))[None, :],
            jnp.tile(attn_k_norm_g[l].astype(F32), 2)[None, :], ones_bd,
            mlstm_conv_w[l].astype(F32), mlstm_conv_b[l].astype(F32)[None, :],
            _pad_lanes(mlstm_b_i[l]), _pad_lanes(mlstm_b_f[l]),
            mlstm_out_norm_g[l].astype(F32)[None, :], wout,
            layer=l, first=(l == 0), last=(l == depth - 1))
    return h
```

```python
import functools

import jax
import jax.numpy as jnp
from jax import lax
from jax.experimental import pallas as pl
from jax.experimental.pallas import tpu as pltpu

D_MODEL = 1024
N_META = 16
HEAD_DIM = 64
ATT_HEADS = 16
KV_HEADS = 2
ATT_BLOCK = 128
ROT_DIM = 16
ROPE_THETA = 500000.0
M_HEADS = 4
M_V_DIM = 256
M_QK_DIM = 128
CHUNK = 128
CONV_K = 4
EPS = 1e-6
NEG = -1e30
LOG2_E = 1.4426950408889634
NEG_LOG2_E = -LOG2_E

LANES = 128
SUBLANES = 8
MXU_COLS = 256
HEAD_PAIRS = ATT_HEADS // 2
PAIRS_PER_KV = HEAD_PAIRS // KV_HEADS
SUB = ATT_BLOCK // 2
WIN = SUB + ATT_BLOCK
KEY_COLS = 2 * WIN + LANES
META_COL = 2 * WIN
SINK_LANE = 2 * N_META

ROW_TILE = 128
LEAD = ROW_TILE
PAD = LEAD - N_META
META_BLK = PAD // ATT_BLOCK
META_OFF = PAD % ATT_BLOCK
NBLK = ROW_TILE // ATT_BLOCK
BOT_ROW = (NBLK + 1) * ATT_BLOCK
META_ROW = 2 * BOT_ROW
KEY_ROWS = META_ROW + ATT_BLOCK

OFF_Q = 0
OFF_K = 1024
OFF_V = 1152
OFF_AG = 1280
OFF_MQ = 2304
OFF_MK = 2816
OFF_MV = 3328
OFF_MO = 4352
OFF_MG = 5376
OFF_GI = 6400
OFF_GF = 6528
IN_COLS = 6656
QK_CONV_COLS = 2 * M_HEADS * M_QK_DIM
MIX_COLS = 2 * D_MODEL

IN_JOB_COLS = MXU_COLS
OUT_JOB_COLS = MXU_COLS
TAIL_JOBS = 2
UNIT_ORDER_A = (0, 1, 3, 5, 7, 9, 12, 10, 14, 16, 18, 20, 22)
UNIT_ORDER_M = (2, 4, 6, 8, 11, 13, 15, 17, 19, 21, 23, 24)
WOUT_PAD_COLS = LANES
VMEM_LIMIT_BYTES = 56 * 1024 * 1024

F32 = jnp.float32
BF16 = jnp.bfloat16


def _sigmoid(x):
    return 1.0 / (1.0 + jnp.exp2(x * NEG_LOG2_E))


def _silu(x):
    return x * _sigmoid(x)


def _log_sigmoid(x):
    return jnp.minimum(x, 0.0) - jnp.log1p(jnp.exp(-jnp.abs(x)))


def _dot(a, b):
    return jnp.dot(a, b, preferred_element_type=F32)


def _dot_nt(a, b):
    return lax.dot_general(a, b, (((1,), (1,)), ((), ())), preferred_element_type=F32)


def _dot_tn(a, b):
    return lax.dot_general(a, b, (((0,), (0,)), ((), ())), preferred_element_type=F32)


def _row_scan(x, row, combine, fill):
    shift = 1
    while shift < x.shape[0]:
        x = combine(x, jnp.where(row >= shift, pltpu.roll(x, shift, axis=0), fill))
        shift *= 2
    return x


def _interleave(a, b):
    out, ia, ib = [], 0, 0
    while ia < len(a) or ib < len(b):
        if ib >= len(b) or (ia < len(a) and ia * len(b) <= ib * len(a)):
            out.append(a[ia]); ia += 1
        else:
            out.append(b[ib]); ib += 1
    return out


def _layer_kernel(sink_ref, hn_ref, h_ref, head_ref, rc_ref, rs_ref, ng_ref, wa_ref, wb_ref, wg_ref,
                  qg_ref, kg_ref, bd_ref,
                  cw_ref, cb_ref, bi_ref, bf_ref, og_ref, wout_ref, o_ref,
                  za_ref, zb_ref, ya_ref, yb_ref, hb_ref, kk_ref, vv_ref, p_ref, u_ref,
                  c_ref, n_ref, m_ref, *, nt, n_tiles, first):
    s = pl.program_id(0)

    @pl.when(s == 0)
    def _():
        zb_ref[...] = jnp.zeros_like(zb_ref)
        yb_ref[...] = jnp.zeros_like(yb_ref)

    @pl.when((s == 0) | (lax.rem(s - 1, nt) == 0))
    def _():
        kk_ref[...] = jnp.zeros_like(kk_ref)
        r = lax.broadcasted_iota(jnp.int32, (KEY_ROWS, LANES), 0)
        l = lax.broadcasted_iota(jnp.int32, (KEY_ROWS, LANES), 1)
        m = r - META_ROW
        top = (r < BOT_ROW) | ((m >= 0) & (m < N_META)) | (m == SINK_LANE)
        bot = ((r >= BOT_ROW) & (r < META_ROW)) | ((m >= N_META) & (m < SINK_LANE)) | (m == SINK_LANE + 1)
        ones = jnp.where(l < HEAD_DIM, jnp.where(top, 1.0, 0.0), jnp.where(bot, 1.0, 0.0)).astype(BF16)
        for c in range(KV_HEADS):
            vv_ref[c, :, 0:LANES] = jnp.zeros((KEY_ROWS, LANES), BF16)
            vv_ref[c, :, LANES:2 * LANES] = ones
        u_ref[ROW_TILE:ROW_TILE + SUBLANES, :] = jnp.zeros((SUBLANES, QK_CONV_COLS), F32)
        c_ref[...] = jnp.zeros_like(c_ref)
        n_ref[...] = jnp.zeros_like(n_ref)
        m_ref[...] = jnp.zeros_like(m_ref)

    step = functools.partial(
        _layer_step, sink_ref, hn_ref, h_ref, head_ref, rc_ref, rs_ref, ng_ref, (wa_ref, wb_ref, wg_ref),
        qg_ref, kg_ref, bd_ref,
        cw_ref, cb_ref, bi_ref, bf_ref, og_ref, wout_ref, o_ref,
        hb_ref, kk_ref, vv_ref, p_ref, u_ref, c_ref, n_ref, m_ref, nt=nt, n_tiles=n_tiles, first=first)

    @pl.when(s % 2 == 0)
    def _():
        step(za_ref, zb_ref, ya_ref, yb_ref)

    @pl.when(s % 2 == 1)
    def _():
        step(zb_ref, za_ref, yb_ref, ya_ref)


def _layer_step(sink_ref, hn_ref, h_ref, head_ref, rc_ref, rs_ref, ng_ref, win_refs, qg_ref, kg_ref, bd_ref,
                cw_ref, cb_ref, bi_ref, bf_ref, og_ref, wout_ref, o_ref,
                hb_ref, kk_ref, vv_ref, p_ref, u_ref, c_ref, n_ref, m_ref,
                zw_ref, z_ref, yw_ref, yr_ref, *, nt, n_tiles, first):
    tb = ROW_TILE
    s = pl.program_id(0)
    i = jnp.where(s < 1, -1, lax.rem(s - 1, nt))
    i_out = jnp.where(s < 2, -1, lax.rem(s - 2, nt))
    nchunk = tb // CHUNK

    hx = hn_ref[0]
    if first:
        hx = jnp.where(lax.rem(jnp.minimum(s, n_tiles - 1), nt) == 0, head_ref[...], hx)
    hb_ref[...] = (hx * lax.rsqrt(jnp.mean(hx * hx, axis=-1, keepdims=True) + EPS)
                   * ng_ref[...]).astype(BF16)
    orow = i_out * tb + lax.broadcasted_iota(jnp.int32, (tb, 1), 0)

    def in_proj_job(c0):
        ref, base = [(r, b) for r, b in zip(win_refs, (0, OFF_MO, OFF_GI)) if b <= c0][-1]

        def run():
            zw_ref[:, c0:c0 + IN_JOB_COLS] = _dot(hb_ref[...], ref[:, c0 - base:c0 - base + IN_JOB_COLS])
        return run

    def out_proj_job(c0):
        def run():
            y = (_dot(yr_ref[:, 0:D_MODEL], wout_ref[0:D_MODEL, c0:c0 + OUT_JOB_COLS])
                 + _dot(yr_ref[:, D_MODEL:MIX_COLS], wout_ref[D_MODEL:MIX_COLS, c0:c0 + OUT_JOB_COLS]))
            res = h_ref[0, :, c0:c0 + OUT_JOB_COLS]
            if first:
                res = jnp.where(i_out == 0, head_ref[:, c0:c0 + OUT_JOB_COLS], res)
            o_ref[0, :, c0:c0 + OUT_JOB_COLS] = jnp.where(orow >= PAD, res + y, 0.0)
        return run

    jobs = _interleave([in_proj_job(c0) for c0 in range(0, IN_COLS, IN_JOB_COLS)],
                       [out_proj_job(c0) for c0 in range(0, D_MODEL, OUT_JOB_COLS)])

    lane = lax.broadcasted_iota(jnp.int32, (ATT_BLOCK, LANES), 1)
    lo_half = lane < HEAD_DIM
    rot_lo = (lane % HEAD_DIM) < (ROT_DIM // 2)

    def head_rms_inv(x):
        w = x.shape[1]
        ss = _dot((x * x).astype(BF16), bd_ref[0:w, 0:w])
        return lax.rsqrt(ss * (1.0 / HEAD_DIM) + EPS)

    def rope(xn, rc, rs):
        partner = jnp.where(rot_lo, pltpu.roll(xn, LANES - ROT_DIM // 2, axis=1),
                            pltpu.roll(xn, ROT_DIM // 2, axis=1))
        return xn * rc + partner * rs

    def keys_of(ref, c, j, r):
        w0 = j * ATT_BLOCK + r * SUB
        return jnp.concatenate([ref[c, w0:w0 + WIN, :], ref[c, BOT_ROW + w0:BOT_ROW + w0 + WIN, :],
                                ref[c, META_ROW:META_ROW + ATT_BLOCK, :]], axis=0)

    attn_units, mlstm_units = [], []

    for j in range(NBLK):
        r0 = j * ATT_BLOCK
        blk = i * NBLK + j
        ctx = {}

        def kv_update(j=j, r0=r0, blk=blk, ctx=ctx):
            rc = rc_ref[r0:r0 + ATT_BLOCK, :]
            rs = rs_ref[r0:r0 + ATT_BLOCK, :]
            kx = z_ref[r0:r0 + ATT_BLOCK, OFF_K:OFF_K + LANES]
            kp = rope(kx * head_rms_inv(kx) * kg_ref[...], rc, rs)
            vp = z_ref[r0:r0 + ATT_BLOCK, OFF_V:OFF_V + LANES]
            kp_sw = pltpu.roll(kp, HEAD_DIM, axis=1)
            vp_sw = pltpu.roll(vp, HEAD_DIM, axis=1)
            slot = (j + 1) * ATT_BLOCK
            for c in range(KV_HEADS):
                for ref, own, swapped in ((kk_ref, kp, kp_sw), (vv_ref, vp, vp_sw)):
                    top = jnp.where(lo_half, own if c == 0 else swapped, 0.0).astype(BF16)
                    bot = jnp.where(lo_half, 0.0, swapped if c == 0 else own).astype(BF16)
                    ref[c, slot:slot + ATT_BLOCK, 0:LANES] = top
                    ref[c, BOT_ROW + slot:BOT_ROW + slot + ATT_BLOCK, 0:LANES] = bot
                    if j == META_BLK % NBLK:
                        for m0, part in ((META_ROW, top), (META_ROW + N_META, bot)):
                            keep = ref[c, m0:m0 + N_META, 0:LANES].astype(F32)
                            new = part[META_OFF:ATT_BLOCK, :].astype(F32)
                            ref[c, m0:m0 + N_META, 0:LANES] = jnp.where(blk == META_BLK, new, keep).astype(BF16)

            sl = lax.broadcasted_iota(jnp.int32, (SUB, LANES), 1)
            si = lax.broadcasted_iota(jnp.int32, (SUB, LANES), 0)
            for r in range(2):
                qrow = blk * ATT_BLOCK + r * SUB + si
                oks = []
                for w in (sl, jnp.where(sl < HEAD_DIM, ATT_BLOCK + sl, sl - HEAD_DIM), sl + HEAD_DIM):
                    krow = (blk - 1) * ATT_BLOCK + r * SUB + w
                    oks.append((krow >= LEAD) & (w > si) & (w <= si + ATT_BLOCK))
                ctx["band_ok", r] = oks
                meta_ok = (PAD + (sl % N_META)) <= qrow
                ctx["meta_ok", r] = ((sl < N_META) & meta_ok, (sl >= N_META) & (sl < 2 * N_META) & meta_ok)
                ctx["meta_any", r] = (sl < 2 * N_META) & meta_ok
            ctx["sink_lanes"] = (sl >> 1) == (SINK_LANE >> 1)
            ctx["first_head_lanes"] = (sl < N_META) | (sl == SINK_LANE)
            ctx["lane"] = sl

        def q_scores(c, j=j, r0=r0, ctx=ctx):
            rc = rc_ref[r0:r0 + ATT_BLOCK, :]
            rs = rs_ref[r0:r0 + ATT_BLOCK, :]
            gain2 = jnp.concatenate([qg_ref[...], qg_ref[...]], axis=1)
            q4 = []
            for half in range(PAIRS_PER_KV // 2):
                c0 = OFF_Q + (c * PAIRS_PER_KV + 2 * half) * LANES
                xq = z_ref[r0:r0 + ATT_BLOCK, c0:c0 + 2 * LANES]
                xn = xq * head_rms_inv(xq) * gain2
                q4.append(rope(xn[:, 0:LANES], rc, rs).astype(BF16))
                q4.append(rope(xn[:, LANES:2 * LANES], rc, rs).astype(BF16))
            for r in range(2):
                q4r = jnp.concatenate([t[r * SUB:(r + 1) * SUB, :] for t in q4], axis=0)
                ctx["s", c, r] = _dot_nt(q4r, keys_of(kk_ref, c, j, r))

        def softmax(c, pi, j=j, ctx=ctx):
            p = c * PAIRS_PER_KV + pi
            pbuf = p_ref.at[j * KV_HEADS + c]
            sl = ctx["lane"]
            lo = sl < HEAD_DIM
            sinks = (sink_ref[2 * p], sink_ref[2 * p + 1])
            sink_pair = jnp.where(sl == SINK_LANE, sinks[0], sinks[1])
            for r in range(2):
                sc = ctx["s", c, r][pi * SUB:(pi + 1) * SUB, :]
                rows = slice((r * PAIRS_PER_KV + pi) * SUB, (r * PAIRS_PER_KV + pi + 1) * SUB)
                b0, b1, b2 = (jnp.where(ok, sc[:, t * LANES:(t + 1) * LANES], NEG)
                              for t, ok in enumerate(ctx["band_ok", r]))
                sm = sc[:, META_COL:META_COL + LANES]
                sm_a = jnp.where(ctx["meta_ok", r][0], sm, NEG)
                sm_b = jnp.where(ctx["meta_ok", r][1], sm, NEG)
                both_a = jnp.maximum(jnp.maximum(b0, jnp.where(lo, b1, NEG)), sm_a)
                both_b = jnp.maximum(jnp.maximum(b2, jnp.where(lo, NEG, b1)), sm_b)
                mx_a = jnp.maximum(jnp.max(both_a, axis=-1, keepdims=True), sinks[0])
                mx_b = jnp.maximum(jnp.max(both_b, axis=-1, keepdims=True), sinks[1])
                pbuf[rows, 0:LANES] = jnp.exp2(b0 - mx_a).astype(BF16)
                pbuf[rows, LANES:2 * LANES] = jnp.exp2(b1 - jnp.where(lo, mx_a, mx_b)).astype(BF16)
                pbuf[rows, 2 * LANES:3 * LANES] = jnp.exp2(b2 - mx_b).astype(BF16)
                sm2 = jnp.where(ctx["meta_any", r], sm, jnp.where(ctx["sink_lanes"], sink_pair, NEG))
                mx2 = jnp.where(ctx["first_head_lanes"], mx_a, mx_b)
                pbuf[rows, META_COL:META_COL + LANES] = jnp.exp2(sm2 - mx2).astype(BF16)

        def pv(c, j=j, r0=r0):
            for r in range(2):
                o4 = _dot(p_ref[j * KV_HEADS + c, r * PAIRS_PER_KV * SUB:(r + 1) * PAIRS_PER_KV * SUB, :],
                          keys_of(vv_ref, c, j, r))
                y0 = r0 + r * SUB
                for pi in range(PAIRS_PER_KV):
                    p = c * PAIRS_PER_KV + pi
                    rows = slice(pi * SUB, (pi + 1) * SUB)
                    gate = _silu(z_ref[y0:y0 + SUB, OFF_AG + p * LANES:OFF_AG + (p + 1) * LANES])
                    yw_ref[y0:y0 + SUB, p * LANES:(p + 1) * LANES] = (
                        o4[rows, 0:LANES] * (1.0 / o4[rows, LANES:2 * LANES]) * gate).astype(BF16)

        attn_units.append((kv_update, 1))
        for c in range(KV_HEADS):
            attn_units.append((functools.partial(q_scores, c), 0))
            attn_units.extend((functools.partial(softmax, c, pi), 1) for pi in range(PAIRS_PER_KV))
            attn_units.append((functools.partial(pv, c), 0))

    def kv_carry():
        for ref in (kk_ref, vv_ref):
            for c in range(KV_HEADS):
                for base in (0, BOT_ROW):
                    ref[c, base:base + ATT_BLOCK, 0:LANES] = (
                        ref[c, base + NBLK * ATT_BLOCK:base + (NBLK + 1) * ATT_BLOCK, 0:LANES])

    crow = lax.broadcasted_iota(jnp.int32, (CHUNK, LANES), 0)
    tril = (lax.broadcasted_iota(jnp.int32, (CHUNK, CHUNK), 0)
            >= lax.broadcasted_iota(jnp.int32, (CHUNK, CHUNK), 1))
    for cj in range(nchunk):
        q0 = cj * CHUNK
        g = {}

        def chunk_prep(cj=cj, q0=q0, g=g):
            if cj == 0:
                u_ref[0:SUBLANES, :] = u_ref[tb:tb + SUBLANES, :]
                u_ref[SUBLANES:SUBLANES + tb, :] = z_ref[:, OFF_MQ:OFF_MQ + QK_CONV_COLS]
            real = (i * tb + q0 + crow) >= PAD
            real_col = (i * tb + q0 + lax.broadcasted_iota(jnp.int32, (CHUNK, 1), 0)) >= PAD
            uu = u_ref[q0:q0 + SUBLANES + CHUNK, :]
            acc = cw_ref[0:1, :] * uu
            for t in range(1, CONV_K):
                acc = cw_ref[t:t + 1, :] * uu + pltpu.roll(acc, 1, axis=0)
            conv = acc[SUBLANES:, :] + cb_ref[...]
            g["qk"] = jnp.where(real_col, _silu(conv), 0.0)

            li = jnp.where(real, z_ref[q0:q0 + CHUNK, OFF_GI:OFF_GI + LANES] + bi_ref[...], NEG)
            lf = jnp.where(real, _log_sigmoid(z_ref[q0:q0 + CHUNK, OFF_GF:OFF_GF + LANES] + bf_ref[...]), 0.0)
            b = _row_scan(lf, crow, jnp.add, 0.0)
            r = li - b
            rmax = jnp.max(r, axis=0, keepdims=True)
            g["wg"] = jnp.exp(r - rmax)
            dmax = b + _row_scan(r, crow, jnp.maximum, NEG)
            m_in = m_ref[0:1, :]
            a_t = b + m_in
            m_t = jnp.maximum(a_t, dmax)
            g["inter"] = jnp.exp(a_t - m_t)
            g["cvec"] = b - m_t
            g["em_t"] = jnp.exp(-m_t)
            g["r_t"] = r.T
            b_last = b[CHUNK - 1:CHUNK, :]
            mg = b_last + rmax
            m_new = jnp.maximum(b_last + m_in, mg)
            g["a_st"] = jnp.exp(b_last + m_in - m_new)
            g["w_st"] = jnp.exp(mg - m_new)
            m_ref[...] = jnp.broadcast_to(m_new, (SUBLANES, LANES))

        def heads_state(q0=q0, g=g):
            for hd in range(M_HEADS):
                qk = g["qk"]
                q = qk[:, hd * M_QK_DIM:(hd + 1) * M_QK_DIM]
                k = qk[:, (M_HEADS + hd) * M_QK_DIM:(M_HEADS + hd + 1) * M_QK_DIM] * (M_QK_DIM ** -0.5)
                vb = z_ref[q0:q0 + CHUNK, OFF_MV + hd * M_V_DIM:OFF_MV + (hd + 1) * M_V_DIM].astype(BF16)
                qb = q.astype(BF16)
                c_in = c_ref[hd]
                n_in = n_ref[hd, 0:1, :]
                g["s", hd] = _dot_nt(qb, k.astype(BF16))
                g["qc", hd] = _dot(qb, c_in.astype(BF16))
                g["qn", hd] = jnp.sum(q * n_in, axis=-1, keepdims=True)
                g["vb", hd] = vb

                kw = g["wg"][:, hd:hd + 1] * k
                a_s = g["a_st"][:, hd:hd + 1]
                w_s = g["w_st"][:, hd:hd + 1]
                c_ref[hd] = a_s * c_in + w_s * _dot_tn(kw.astype(BF16), vb)
                n_new = a_s * n_in + w_s * jnp.sum(kw, axis=0, keepdims=True)
                n_ref[hd] = jnp.broadcast_to(n_new, (SUBLANES, LANES))

        def head_out(hd, q0=q0, g=g):
            arg = g["cvec"][:, hd:hd + 1] + g["r_t"][hd:hd + 1, :]
            sg = g["s", hd] * jnp.exp(jnp.where(tril, arg, NEG))
            inter_c = g["inter"][:, hd:hd + 1]
            num = inter_c * g["qc", hd] + _dot(sg.astype(BF16), g["vb", hd])
            den = inter_c * g["qn", hd] + jnp.sum(sg, axis=-1, keepdims=True)
            hout = num / jnp.maximum(jnp.abs(den), g["em_t"][:, hd:hd + 1])

            cols = slice(hd * M_V_DIM, (hd + 1) * M_V_DIM)
            hm = hout * lax.rsqrt(jnp.mean(hout * hout, axis=-1, keepdims=True) + EPS) * og_ref[:, cols]
            ocols = slice(OFF_MO + hd * M_V_DIM, OFF_MO + (hd + 1) * M_V_DIM)
            gcols = slice(OFF_MG + hd * M_V_DIM, OFF_MG + (hd + 1) * M_V_DIM)
            ym = hm * _sigmoid(z_ref[q0:q0 + CHUNK, ocols]) * _silu(z_ref[q0:q0 + CHUNK, gcols])
            yw_ref[q0:q0 + CHUNK, D_MODEL + hd * M_V_DIM:D_MODEL + (hd + 1) * M_V_DIM] = ym.astype(BF16)

        mlstm_units.append((chunk_prep, 1))
        mlstm_units.append((heads_state, 0))
        mlstm_units.extend((functools.partial(head_out, hd), 1) for hd in range(M_HEADS))

    kv_carry()
    if len(attn_units) == len(UNIT_ORDER_A) and len(mlstm_units) == len(UNIT_ORDER_M):
        order = sorted([(pos, u) for pos, u in zip(UNIT_ORDER_A, attn_units)]
                       + [(pos, u) for pos, u in zip(UNIT_ORDER_M, mlstm_units)], key=lambda e: e[0])
        units = [u for _, u in order]
    else:
        units = _interleave(attn_units, mlstm_units)
    n_receivers = sum(takes for _, takes in units)
    n_spread = len(jobs) - TAIL_JOBS
    seen = 0
    for unit, takes in units:
        seen += takes
        while takes and len(jobs) > TAIL_JOBS and (n_spread + TAIL_JOBS - len(jobs)) * n_receivers < seen * n_spread:
            jobs.pop(0)()
        unit()
    for job in jobs:
        job()


def _layer_call(h, head, sink, rc, rs, ng, win, qg, kg, bd, cw, cb, bi, bf, og, wout, *, layer, first, last):
    batch, rows, d = h.shape
    tb = ROW_TILE
    nt = (rows + LEAD if first else rows) // tb
    last_tile = batch * nt - 1

    def tile_of(t):
        t = jnp.clip(t, 0, last_tile)
        return t // nt, t % nt, 0

    def token_tile_of(t):
        b, i, _ = tile_of(t)
        return b, jnp.maximum(i - LEAD // tb, 0), 0

    const = lambda s: (0, 0)
    in_tile = token_tile_of if first else tile_of
    mix_rows = lambda s: tile_of(s - 1)[1:]
    out_tile = token_tile_of if last else tile_of
    out_rows = rows + (LEAD if first else 0) - (LEAD if last else 0)
    once = pl.Buffered(1)
    full = lambda a: pl.BlockSpec(a.shape, const)
    in_specs = [
        pl.BlockSpec(memory_space=pltpu.SMEM),
        pl.BlockSpec((1, tb, d), lambda s: in_tile(s)),
        pl.BlockSpec((1, tb, d), lambda s: in_tile(s - 2)),
        full(head),
        pl.BlockSpec((tb, LANES), mix_rows),
        pl.BlockSpec((tb, LANES), mix_rows),
        full(ng),
        *[pl.BlockSpec((None,) + w.shape[1:], lambda s: (layer, 0, 0), pipeline_mode=once) for w in win],
        full(qg), full(kg), full(bd), full(cw), full(cb), full(bi), full(bf), full(og),
        pl.BlockSpec((None,) + wout.shape[1:], lambda s: (layer, 0, 0), pipeline_mode=once),
    ]
    scratch = [
        pltpu.VMEM((tb, IN_COLS), F32),
        pltpu.VMEM((tb, IN_COLS), F32),
        pltpu.VMEM((tb, MIX_COLS), BF16),
        pltpu.VMEM((tb, MIX_COLS), BF16),
        pltpu.VMEM((tb, D_MODEL), BF16),
        pltpu.VMEM((KV_HEADS, KEY_ROWS, LANES), BF16),
        pltpu.VMEM((KV_HEADS, KEY_ROWS, 2 * LANES), BF16),
        pltpu.VMEM((NBLK * KV_HEADS, PAIRS_PER_KV * ATT_BLOCK, KEY_COLS), BF16),
        pltpu.VMEM((tb + SUBLANES, QK_CONV_COLS), F32),
        pltpu.VMEM((M_HEADS, M_QK_DIM, M_V_DIM), F32),
        pltpu.VMEM((M_HEADS, SUBLANES, LANES), F32),
        pltpu.VMEM((SUBLANES, LANES), F32),
    ]
    return pl.pallas_call(
        functools.partial(_layer_kernel, nt=nt, n_tiles=batch * nt, first=first),
        out_shape=jax.ShapeDtypeStruct((batch, out_rows, d), h.dtype),
        grid=(batch * nt + 2,),
        in_specs=in_specs,
        out_specs=pl.BlockSpec((1, tb, d), lambda s: out_tile(s - 2)),
        scratch_shapes=scratch,
        compiler_params=pltpu.CompilerParams(
            dimension_semantics=("arbitrary",),
            vmem_limit_bytes=VMEM_LIMIT_BYTES),
        name="hybrid_layer",
    )(sink, h, h, head, rc, rs, ng, *win, qg, kg, bd, cw, cb, bi, bf, og, wout)


def _rope_tables(lp):
    pos = jnp.arange(lp, dtype=F32) - PAD
    inv_freq = ROPE_THETA ** (-jnp.arange(0, ROT_DIM, 2, dtype=F32) / ROT_DIM)
    ang = pos[:, None] * inv_freq[None, :]
    cos, sin = jnp.cos(ang), jnp.sin(ang)
    ones = jnp.ones((lp, HEAD_DIM - ROT_DIM), F32)
    zeros = jnp.zeros((lp, HEAD_DIM - ROT_DIM), F32)
    rc = jnp.concatenate([cos, cos, ones], axis=1)
    rs = jnp.concatenate([-sin, sin, zeros], axis=1)
    return jnp.tile(rc, (1, 2)), jnp.tile(rs, (1, 2))


def _pad_lanes(v):
    return jnp.pad(v.astype(F32), (0, LANES - v.shape[0]))[None, :]


def kernel(x, meta, norm_g, w_in, attn_q_norm_g, attn_k_norm_g, attn_sink, mlstm_conv_w,
           mlstm_conv_b, mlstm_b_i, mlstm_b_f, mlstm_out_norm_g, w_out):
    batch, seq, d = x.shape
    depth = w_in.shape[0]
    assert d == D_MODEL and seq % ROW_TILE == 0
    lp = seq + LEAD
    head = jnp.concatenate([jnp.zeros((PAD, d), x.dtype), meta.astype(x.dtype)], axis=0)
    h = x
    rc, rs = _rope_tables(lp)
    head_id = jnp.arange(2 * LANES) // HEAD_DIM
    ones_bd = (head_id[:, None] == head_id[None, :]).astype(BF16)

    gate_lo = OFF_MO
    n_gate = 2 * M_HEADS
    lane_pad = ((0, 0), (0, 0), (0, LANES - M_HEADS))
    gi = jnp.pad(w_in[:, :, gate_lo:gate_lo + M_HEADS], lane_pad)
    gf = jnp.pad(w_in[:, :, gate_lo + M_HEADS:gate_lo + n_gate], lane_pad)
    stride_pad = ((0, 0), (0, 0), (0, WOUT_PAD_COLS))
    win = (w_in[:, :, :gate_lo].astype(BF16),
           jnp.pad(w_in[:, :, gate_lo + n_gate:].astype(BF16), stride_pad),
           jnp.pad(jnp.concatenate([gi, gf], axis=2).astype(BF16), stride_pad))
    wout = jnp.pad(w_out.astype(BF16), ((0, 0), (0, 0), (0, WOUT_PAD_COLS)))
    for l in range(depth):
        h = _layer_call(
            h, head, attn_sink[l].astype(F32) * LOG2_E, rc, rs,
            norm_g[l].astype(F32)[None, :], win,
            (jnp.tile(attn_q_norm_g[l].astype(F32), 2) * (HEAD_DIM ** -0.5 * LOG2_E))[None, :],
            jnp.tile(attn_k_norm_g[l].astype(F32), 2)[None, :], ones_bd,
            mlstm_conv_w[l].astype(F32), mlstm_conv_b[l].astype(F32)[None, :],
            _pad_lanes(mlstm_b_i[l]), _pad_lanes(mlstm_b_f[l]),
            mlstm_out_norm_g[l].astype(F32)[None, :], wout,
            layer=l, first=(l == 0), last=(l == depth - 1))
    return h
```
